```python
import jax, jax.numpy as jnp
from jax import lax
import numpy as np

D_MODEL = 1024
BATCH = 32
SEQ = 2048
DEPTH = 1

CHUNK = 64
HEAD_DIM = 64
MIX_WIDTH = D_MODEL
A_WIDTH = MIX_WIDTH // 2
B_WIDTH = MIX_WIDTH - A_WIDTH
A_HEADS = A_WIDTH // HEAD_DIM
A_KV_HEADS = max(1, A_HEADS // 4)
B_HEADS = B_WIDTH // HEAD_DIM
WINDOW = 128
A_PREV_CHUNKS = WINDOW // CHUNK
B_PREV_CHUNKS = 8
MAX_REL = 128
PLE_DIM = 256
RMS_EPS = 1e-6
NEG_BIG = -1e30

A_Q = A_HEADS * HEAD_DIM
A_KV = A_KV_HEADS * HEAD_DIM
PROJ_SIZES = (A_Q, A_KV, A_KV, A_WIDTH, B_WIDTH, B_WIDTH, B_WIDTH, B_WIDTH)
D_IN_PROJ = sum(PROJ_SIZES)
SPLIT_POINTS = [int(v) for v in np.cumsum(PROJ_SIZES)[:-1]]

kernel_name = "hybrid_chunk_swa_sink_relbias_ple"


def rmsnorm(x, g):
    xf = x.astype(jnp.float32)
    var = jnp.mean(xf * xf, axis=-1, keepdims=True)
    return (xf * lax.rsqrt(var + RMS_EPS)).astype(x.dtype) * g


def band_rel(n_prev):
    band = (n_prev + 1) * CHUNK
    qi = jnp.arange(CHUNK, dtype=jnp.int32)[:, None]
    kj = jnp.arange(band, dtype=jnp.int32)[None, :]
    return qi - kj + n_prev * CHUNK


def alibi_bias(n_heads, n_prev):
    slopes = jnp.asarray(2.0 ** (-8.0 * np.arange(1, n_heads + 1) / n_heads), dtype=jnp.float32)
    dist = jnp.abs(band_rel(n_prev)).astype(jnp.float32)
    return -slopes[:, None, None] * dist[None]


def rel_position_bias(table, n_prev):
    idx = jnp.clip(band_rel(n_prev), -MAX_REL, MAX_REL) + MAX_REL
    return table.astype(jnp.float32)[:, idx]


def chunk_band_attention(q, k, v, n_prev, bias, sink):
    b, s, hq, d = q.shape
    hkv = k.shape[2]
    grp = hq // hkv
    nc = s // CHUNK
    band = (n_prev + 1) * CHUNK
    pad = n_prev * CHUNK
    kp = jnp.pad(k, ((0, 0), (pad, 0), (0, 0), (0, 0)))
    vp = jnp.pad(v, ((0, 0), (pad, 0), (0, 0), (0, 0)))
    qc = q.reshape(b, nc, CHUNK, hkv, grp, d)
    bias_g = bias.reshape(hkv, grp, CHUNK, band)
    scale = HEAD_DIM ** -0.5
    if sink is not None:
        sink_g = sink.astype(jnp.float32).reshape(hkv, grp, 1, 1)

    def one_chunk(c):
        qb = lax.dynamic_index_in_dim(qc, c, axis=1, keepdims=False)
        kb = lax.dynamic_slice_in_dim(kp, c * CHUNK, band, axis=1)
        vb = lax.dynamic_slice_in_dim(vp, c * CHUNK, band, axis=1)
        sc = jnp.einsum('bqkgd,bskd->bkgqs', qb, kb).astype(jnp.float32) * scale + bias_g
        valid = jnp.arange(band) >= pad - c * CHUNK
        sc = jnp.where(valid, sc, NEG_BIG)
        m = jnp.max(sc, axis=-1, keepdims=True)
        if sink is not None:
            m = jnp.maximum(m, sink_g)
            e = jnp.exp(sc - m)
            denom = jnp.sum(e, axis=-1, keepdims=True) + jnp.exp(sink_g - m)
        else:
            e = jnp.exp(sc - m)
            denom = jnp.sum(e, axis=-1, keepdims=True)
        w = (e / denom).astype(vb.dtype)
        out = jnp.einsum('bkgqs,bskd->bqkgd', w, vb)
        return out.reshape(b, CHUNK, hq * d)

    out = lax.map(one_chunk, jnp.arange(nc))
    return jnp.transpose(out, (1, 0, 2, 3)).reshape(b, s, hq * d)


def setup_inputs(seed: int = 0) -> dict:
    key = jax.random.key(seed)
    ks = jax.random.split(key, 12)
    f32 = jnp.float32
    x = jax.random.normal(ks[0], (BATCH, SEQ, D_MODEL), f32)
    p = jax.random.normal(ks[1], (DEPTH, BATCH, SEQ, PLE_DIM), f32)
    norm_g = 1.0 + 0.02 * jax.random.normal(ks[2], (DEPTH, D_MODEL), f32)
    w_in = jax.random.normal(ks[3], (DEPTH, D_MODEL, D_IN_PROJ), f32) * D_MODEL ** -0.5
    sink_a = 0.5 * jax.random.normal(ks[4], (DEPTH, A_HEADS), f32)
    rel_bias_b = 0.1 * jax.random.normal(ks[5], (DEPTH, B_HEADS, 2 * MAX_REL + 1), f32)
    w_out = jax.random.normal(ks[6], (DEPTH, MIX_WIDTH, D_MODEL), f32) * MIX_WIDTH ** -0.5
    ple_norm_g = 1.0 + 0.02 * jax.random.normal(ks[7], (DEPTH, D_MODEL), f32)
    w_ple_proj = jax.random.normal(ks[8], (DEPTH, PLE_DIM, D_MODEL), f32) * PLE_DIM ** -0.5
    w_ple_gate = jax.random.normal(ks[9], (DEPTH, D_MODEL, D_MODEL), f32) * D_MODEL ** -0.5
    final_norm_g = 1.0 + 0.02 * jax.random.normal(ks[10], (D_MODEL,), f32)
    return {"x": x, "p": p, "norm_g": norm_g, "w_in": w_in, "sink_a": sink_a,
            "rel_bias_b": rel_bias_b, "w_out": w_out, "ple_norm_g": ple_norm_g,
            "w_ple_proj": w_ple_proj, "w_ple_gate": w_ple_gate, "final_norm_g": final_norm_g}


def reference(x, p, norm_g, w_in, sink_a, rel_bias_b, w_out, ple_norm_g, w_ple_proj, w_ple_gate, final_norm_g):
    b, s, _ = x.shape
    bias_a = alibi_bias(A_HEADS, A_PREV_CHUNKS)
    h = x
    for i in range(DEPTH):
        u = rmsnorm(h, norm_g[i])
        z = u @ w_in[i]
        qa, ka, va, ga, qb, kb, vb, gb = jnp.split(z, SPLIT_POINTS, axis=-1)
        ya = chunk_band_attention(
            qa.reshape(b, s, A_HEADS, HEAD_DIM),
            ka.reshape(b, s, A_KV_HEADS, HEAD_DIM),
            va.reshape(b, s, A_KV_HEADS, HEAD_DIM),
            A_PREV_CHUNKS, bias_a, sink_a[i])
        bias_b = rel_position_bias(rel_bias_b[i], B_PREV_CHUNKS)
        yb = chunk_band_attention(
            qb.reshape(b, s, B_HEADS, HEAD_DIM),
            kb.reshape(b, s, B_HEADS, HEAD_DIM),
            vb.reshape(b, s, B_HEADS, HEAD_DIM),
            B_PREV_CHUNKS, bias_b, None)
        y = jnp.concatenate([ya * jax.nn.silu(ga), yb * jax.nn.silu(gb)], axis=-1)
        h = h + y @ w_out[i]
        gate = jax.nn.sigmoid(rmsnorm(h, ple_norm_g[i]) @ w_ple_gate[i])
        h = h + (p[i] @ w_ple_proj[i]) * gate
    return rmsnorm(h, final_norm_g)
```

```python
import functools

import jax
import jax.numpy as jnp
import numpy as np
from jax import lax
from jax.experimental import pallas as pl
from jax.experimental.pallas import tpu as pltpu

D_MODEL = 1024
CHUNK = 64
HEAD_DIM = 64
A_HEADS = 8
A_KV_HEADS = 2
B_HEADS = 8
A_PREV = 2
B_PREV = 8
MAX_REL = 128
RMS_EPS = 1e-6
NEG_BIG = -1e30

A_Q = A_HEADS * HEAD_DIM
A_KV = A_KV_HEADS * HEAD_DIM
A_W = 512
B_W = 512
OFF_QA, OFF_KA, OFF_VA, OFF_GA = 0, 512, 640, 768
OFF_QB, OFF_KB, OFF_VB, OFF_GB = 1280, 1792, 2304, 2816
D_IN_PROJ = 3328

LANES = 128
PAIR = 2 * CHUNK
A_BAND = (A_PREV + 2) * CHUNK
B_BAND = (B_PREV + 2) * CHUNK
KV_PAD = B_PREV * CHUNK
ZQ_W = 2048
ZKV_W = 1536
TOK_BLOCK = 512
VMEM_LIMIT = 56 * 1024 * 1024
Q_SCALE = HEAD_DIM ** -0.5


def _rms_scale(xf):
    var = jnp.mean(xf * xf, axis=-1, keepdims=True)
    return xf * lax.rsqrt(var + RMS_EPS)


def _dup_halves(v):
    rolled = pltpu.roll(v, HEAD_DIM, axis=1)
    lane = lax.broadcasted_iota(jnp.int32, v.shape, 1)
    lo = lane < HEAD_DIM
    return jnp.where(lo, v, rolled), jnp.where(lo, rolled, v)


def _in_proj_kernel(x_ref, g_ref, w_ref, zq_ref, zkv_ref):
    s = pl.program_id(1)

    @pl.when(s == 0)
    def _():
        zkv_ref[0] = jnp.zeros(zkv_ref.shape[1:], zkv_ref.dtype)

    @pl.when(s > 0)
    def _():
        u = (_rms_scale(x_ref[0]) * g_ref[...]).astype(jnp.bfloat16)

        def proj(off, width):
            return jnp.dot(u, w_ref[:, off:off + width], preferred_element_type=jnp.float32)

        bf = jnp.bfloat16
        zq_ref[0, :, 0:512] = (proj(OFF_QA, A_Q) * Q_SCALE).astype(bf)
        zq_ref[0, :, 512:1024] = proj(OFF_GA, A_W).astype(bf)
        zq_ref[0, :, 1024:1536] = (proj(OFF_QB, B_W) * Q_SCALE).astype(bf)
        zq_ref[0, :, 1536:2048] = proj(OFF_GB, B_W).astype(bf)
        kd0, kd1 = _dup_halves(proj(OFF_KA, A_KV))
        vd0, vd1 = _dup_halves(proj(OFF_VA, A_KV))
        zkv_ref[0, :, 0:128] = kd0.astype(bf)
        zkv_ref[0, :, 128:256] = kd1.astype(bf)
        zkv_ref[0, :, 256:384] = vd0.astype(bf)
        zkv_ref[0, :, 384:512] = vd1.astype(bf)
        zkv_ref[0, :, 512:1024] = proj(OFF_KB, B_W).astype(bf)
        zkv_ref[0, :, 1024:1536] = proj(OFF_VB, B_W).astype(bf)


def _in_proj(x, norm_g, w_in_bf16):
    b, s, d = x.shape
    nblk = s // TOK_BLOCK
    pad_blk = KV_PAD // TOK_BLOCK
    assert pad_blk * TOK_BLOCK == KV_PAD and nblk * TOK_BLOCK == s
    tok_idx = lambda bi, si: (bi, jnp.maximum(si - pad_blk, 0), 0)
    return pl.pallas_call(
        _in_proj_kernel,
        grid=(b, nblk + pad_blk),
        in_specs=[
            pl.BlockSpec((1, TOK_BLOCK, d), tok_idx),
            pl.BlockSpec((1, d), lambda bi, si: (0, 0)),
            pl.BlockSpec((d, D_IN_PROJ), lambda bi, si: (0, 0)),
        ],
        out_specs=[
            pl.BlockSpec((1, TOK_BLOCK, ZQ_W), tok_idx),
            pl.BlockSpec((1, TOK_BLOCK, ZKV_W), lambda bi, si: (bi, si, 0)),
        ],
        out_shape=[
            jax.ShapeDtypeStruct((b, s, ZQ_W), jnp.bfloat16),
            jax.ShapeDtypeStruct((b, s + KV_PAD, ZKV_W), jnp.bfloat16),
        ],
        compiler_params=pltpu.CompilerParams(
            dimension_semantics=("arbitrary", "arbitrary"),
            vmem_limit_bytes=VMEM_LIMIT),
        name="in_proj",
    )(x, norm_g, w_in_bf16)


def _softmax_pv(s, v, sink):
    m = jnp.max(s, axis=-1, keepdims=True)
    if sink is not None:
        m = jnp.maximum(m, sink)
    e = jnp.exp(s - m)
    denom = jnp.sum(e, axis=-1, keepdims=True)
    if sink is not None:
        denom = denom + jnp.exp(sink - m)
    o = jnp.dot(e.astype(jnp.bfloat16), v, preferred_element_type=jnp.float32)
    return o / denom


def _silu(g):
    return g * (1.0 / (1.0 + jnp.exp(-g)))


def _attn_kernel(zq_ref, zkv_ref, bias_a_ref, bias_b_ref, start_a_ref, start_b_ref,
                 sink_ref, y_ref):
    blk = pl.program_id(1)
    pairs_per_blk = TOK_BLOCK // PAIR
    lane = lax.broadcasted_iota(jnp.int32, (PAIR, LANES), 1)
    lo = lane < HEAD_DIM
    zero = jnp.zeros((), jnp.bfloat16)

    def split_heads(qp):
        return jnp.concatenate([jnp.where(lo, qp, zero), jnp.where(lo, zero, qp)], axis=0)

    def merge_heads(o):
        return jnp.where(lo, o[0:PAIR], o[PAIR:2 * PAIR])

    def pair_body(pp, carry):
        row0 = pl.multiple_of(pp * PAIR, PAIR)
        gp = blk * pairs_per_blk + pp
        band0 = pl.multiple_of(gp * PAIR, PAIR)
        rows = pl.ds(row0, PAIR)

        a_rows = pl.ds(band0 + (KV_PAD - A_PREV * CHUNK), A_BAND)
        start_a = start_a_ref[pl.ds(gp, 1), :]
        for g in range(A_KV_HEADS):
            q0 = zq_ref[0, rows, 256 * g:256 * g + 128]
            q1 = zq_ref[0, rows, 256 * g + 128:256 * g + 256]
            lhs = jnp.concatenate([split_heads(q0), split_heads(q1)], axis=0)
            k = zkv_ref[0, a_rows, 128 * g:128 * g + 128]
            v = zkv_ref[0, a_rows, 256 + 128 * g:256 + 128 * g + 128]
            s = lax.dot_general(lhs, k, (((1,), (1,)), ((), ())),
                                preferred_element_type=jnp.float32)
            s = s + bias_a_ref[g] + start_a
            o = _softmax_pv(s, v, sink_ref[g])
            out = jnp.concatenate([merge_heads(o[0:2 * PAIR]), merge_heads(o[2 * PAIR:4 * PAIR])],
                                  axis=1)
            gate = zq_ref[0, rows, 512 + 256 * g:512 + 256 * g + 256].astype(jnp.float32)
            y_ref[0, rows, 256 * g:256 * g + 256] = (out * _silu(gate)).astype(y_ref.dtype)

        b_rows = pl.ds(band0, B_BAND)
        start_b = start_b_ref[pl.ds(gp, 1), :]
        for hp in range(B_HEADS // 2):
            qp = zq_ref[0, rows, 1024 + 128 * hp:1024 + 128 * hp + 128]
            k = zkv_ref[0, b_rows, 512 + 128 * hp:512 + 128 * hp + 128]
            v = zkv_ref[0, b_rows, 1024 + 128 * hp:1024 + 128 * hp + 128]
            s = lax.dot_general(split_heads(qp), k, (((1,), (1,)), ((), ())),
                                preferred_element_type=jnp.float32)
            s = s + bias_b_ref[hp] + start_b
            out = merge_heads(_softmax_pv(s, v, None))
            gate = zq_ref[0, rows, 1536 + 128 * hp:1536 + 128 * hp + 128].astype(jnp.float32)
            y_ref[0, rows, 512 + 128 * hp:512 + 128 * hp + 128] = (
                out * _silu(gate)).astype(y_ref.dtype)
        return carry

    lax.fori_loop(0, pairs_per_blk, pair_body, 0)


def _pair_bias_tables(sink_a, rel_bias_b):
    i = np.arange(PAIR)[:, None]
    ci = i // CHUNK
    jk = np.arange(A_BAND)[None, :]
    kc = jk // CHUNK
    rel_a = i - jk + A_PREV * CHUNK
    valid_a = (kc >= ci) & (kc <= ci + A_PREV)
    slopes = 2.0 ** (-8.0 * np.arange(1, A_HEADS + 1) / A_HEADS)
    bias_a = np.where(valid_a[None], -slopes[:, None, None] * np.abs(rel_a)[None], NEG_BIG)
    bias_a = jnp.asarray(bias_a.reshape(A_KV_HEADS, 4 * PAIR, A_BAND), jnp.float32)
    jk = np.arange(B_BAND)[None, :]
    kc = jk // CHUNK
    rel_b = i - jk + B_PREV * CHUNK
    valid_b = (kc >= ci) & (kc <= ci + B_PREV)
    idx = np.clip(rel_b, -MAX_REL, MAX_REL) + MAX_REL
    tbl = rel_bias_b.astype(jnp.float32)[:, idx]
    bias_b = jnp.where(jnp.asarray(valid_b)[None], tbl, NEG_BIG)
    bias_b = bias_b.reshape(B_HEADS // 2, 2 * PAIR, B_BAND)
    sink = jnp.repeat(sink_a.astype(jnp.float32), PAIR).reshape(A_KV_HEADS, 4 * PAIR, 1)
    return bias_a, bias_b, sink


def _start_masks(n_pairs):
    gp = np.arange(n_pairs)[:, None]
    ra = np.arange(A_BAND)[None, :]
    rb = np.arange(B_BAND)[None, :]
    start_a = np.where(PAIR * gp - A_PREV * CHUNK + ra < 0, NEG_BIG, 0.0)
    start_b = np.where(PAIR * gp - B_PREV * CHUNK + rb < 0, NEG_BIG, 0.0)
    return jnp.asarray(start_a, jnp.float32), jnp.asarray(start_b, jnp.float32)


def _attention(zq, zkv, bias_a, bias_b, start_a, start_b, sink):
    b, s, _ = zq.shape
    nblk = s // TOK_BLOCK
    const3 = lambda bi, si: (0, 0, 0)
    const2 = lambda bi, si: (0, 0)
    return pl.pallas_call(
        _attn_kernel,
        grid=(b, nblk),
        in_specs=[
            pl.BlockSpec((1, TOK_BLOCK, ZQ_W), lambda bi, si: (bi, si, 0)),
            pl.BlockSpec((1, s + KV_PAD, ZKV_W), lambda bi, si: (bi, 0, 0)),
            pl.BlockSpec(bias_a.shape, const3),
            pl.BlockSpec(bias_b.shape, const3),
            pl.BlockSpec(start_a.shape, const2),
            pl.BlockSpec(start_b.shape, const2),
            pl.BlockSpec(sink.shape, const3),
        ],
        out_specs=pl.BlockSpec((1, TOK_BLOCK, D_MODEL), lambda bi, si: (bi, si, 0)),
        out_shape=jax.ShapeDtypeStruct((b, s, D_MODEL), jnp.bfloat16),
        compiler_params=pltpu.CompilerParams(
            dimension_semantics=("arbitrary", "arbitrary"),
            vmem_limit_bytes=VMEM_LIMIT),
        name="attention",
    )(zq, zkv, bias_a, bias_b, start_a, start_b, sink)


def _out_proj_kernel(x_ref, y_ref, p_ref, w_out_ref, ple_g_ref, w_gate_ref, w_proj_ref,
                     fin_g_ref, o_ref):
    h = x_ref[...] + jnp.dot(y_ref[...], w_out_ref[...], preferred_element_type=jnp.float32)
    u = (_rms_scale(h) * ple_g_ref[...]).astype(jnp.bfloat16)
    gl = jnp.dot(u, w_gate_ref[...], preferred_element_type=jnp.float32)
    gate = 1.0 / (1.0 + jnp.exp(-gl))
    pe = jnp.dot(p_ref[...].astype(jnp.bfloat16), w_proj_ref[...],
                 preferred_element_type=jnp.float32)
    h = h + pe * gate
    o_ref[...] = _rms_scale(h) * fin_g_ref[...]


def _out_proj(x2, y2, p2, w_out, ple_g, w_gate, w_proj, fin_g):
    n, d = x2.shape
    pd = p2.shape[1]
    tok = lambda i: (i, 0)
    const = lambda i: (0, 0)
    return pl.pallas_call(
        _out_proj_kernel,
        grid=(n // TOK_BLOCK,),
        in_specs=[
            pl.BlockSpec((TOK_BLOCK, d), tok),
            pl.BlockSpec((TOK_BLOCK, d), tok),
            pl.BlockSpec((TOK_BLOCK, pd), tok),
            pl.BlockSpec((d, d), const),
            pl.BlockSpec((1, d), const),
            pl.BlockSpec((d, d), const),
            pl.BlockSpec((pd, d), const),
            pl.BlockSpec((1, d), const),
        ],
        out_specs=pl.BlockSpec((TOK_BLOCK, d), tok),
        out_shape=jax.ShapeDtypeStruct((n, d), jnp.float32),
        compiler_params=pltpu.CompilerParams(
            dimension_semantics=("arbitrary",),
            vmem_limit_bytes=VMEM_LIMIT),
        name="out_proj",
    )(x2, y2, p2, w_out, ple_g, w_gate, w_proj, fin_g)


@jax.jit
def kernel(x, p, norm_g, w_in, sink_a, rel_bias_b, w_out, ple_norm_g, w_ple_proj, w_ple_gate,
           final_norm_g):
    b, s, d = x.shape
    bf = jnp.bfloat16
    start_a, start_b = _start_masks(s // PAIR)
    h = x
    for i in range(norm_g.shape[0]):
        zq, zkv = _in_proj(h, norm_g[i][None], w_in[i].astype(bf))
        bias_a, bias_b, sink = _pair_bias_tables(sink_a[i], rel_bias_b[i])
        y = _attention(zq, zkv, bias_a, bias_b, start_a, start_b, sink)
        is_last = i == norm_g.shape[0] - 1
        assert is_last, "final RMSNorm is fused into the last layer's output kernel"
        h = _out_proj(h.reshape(b * s, d), y.reshape(b * s, d), p[i].reshape(b * s, -1),
                      w_out[i].astype(bf), ple_norm_g[i][None], w_ple_gate[i].astype(bf),
                      w_ple_proj[i].astype(bf), final_norm_g[None]).reshape(b, s, d)
    return h
```

```python
import math

import jax
import jax.numpy as jnp
import numpy as np
from jax import lax
from jax.experimental import pallas as pl
from jax.experimental.pallas import tpu as pltpu

D_MODEL = 1024
CHUNK = 64
HEAD_DIM = 64
A_HEADS = 8
A_KV_HEADS = 2
A_GROUP = A_HEADS // A_KV_HEADS
B_HEADS = 8
A_PREV = 2
B_PREV = 8
MAX_REL = 128
RMS_EPS = 1e-6
NEG_BIG = -1e30
LOG2E = math.log2(math.e)

A_Q = A_HEADS * HEAD_DIM
A_KV = A_KV_HEADS * HEAD_DIM
A_W = 512
B_W = 512
OFF_QA, OFF_KA, OFF_VA, OFF_GA = 0, 512, 640, 768
OFF_QB, OFF_KB, OFF_VB, OFF_GB = 1280, 1792, 2304, 2816
D_IN_PROJ = 3328

LANES = 128
PAIR = 2 * CHUNK
A_BAND = (A_PREV + 2) * CHUNK
B_BAND = (B_PREV + 2) * CHUNK
KV_PAD = B_PREV * CHUNK
A_START_VARIANTS = A_PREV * CHUNK // PAIR + 1
B_START_VARIANTS = B_PREV * CHUNK // PAIR + 1
TOEP_W = B_BAND + PAIR
ZQ_W = 2048
ZKV_W = 1536
TOK_BLOCK = 512
VMEM_LIMIT = 56 * 1024 * 1024
Q_SCALE = HEAD_DIM ** -0.5 * LOG2E


def _rms_scale(xf):
    var = jnp.mean(xf * xf, axis=-1, keepdims=True)
    return xf * lax.rsqrt(var + RMS_EPS)


def _dup_halves(v):
    rolled = pltpu.roll(v, HEAD_DIM, axis=1)
    lane = lax.broadcasted_iota(jnp.int32, v.shape, 1)
    lo = lane < HEAD_DIM
    return jnp.where(lo, v, rolled), jnp.where(lo, rolled, v)


def _in_proj_kernel(x_ref, g_ref, w_ref, zq_ref, zkv_ref):
    s = pl.program_id(1)

    @pl.when(s == 0)
    def _():
        zkv_ref[0] = jnp.zeros(zkv_ref.shape[1:], zkv_ref.dtype)

    @pl.when(s > 0)
    def _():
        u = (_rms_scale(x_ref[0]) * g_ref[...]).astype(jnp.bfloat16)

        def proj(off, width):
            return jnp.dot(u, w_ref[:, off:off + width], preferred_element_type=jnp.float32)

        bf = jnp.bfloat16
        zq_ref[0, :, 0:512] = (proj(OFF_QA, A_Q) * Q_SCALE).astype(bf)
        zq_ref[0, :, 512:1024] = proj(OFF_GA, A_W).astype(bf)
        zq_ref[0, :, 1024:1536] = (proj(OFF_QB, B_W) * Q_SCALE).astype(bf)
        zq_ref[0, :, 1536:2048] = proj(OFF_GB, B_W).astype(bf)
        kd0, kd1 = _dup_halves(proj(OFF_KA, A_KV))
        vd0, vd1 = _dup_halves(proj(OFF_VA, A_KV))
        zkv_ref[0, :, 0:128] = kd0.astype(bf)
        zkv_ref[0, :, 128:256] = kd1.astype(bf)
        zkv_ref[0, :, 256:384] = vd0.astype(bf)
        zkv_ref[0, :, 384:512] = vd1.astype(bf)
        zkv_ref[0, :, 512:1024] = proj(OFF_KB, B_W).astype(bf)
        zkv_ref[0, :, 1024:1536] = proj(OFF_VB, B_W).astype(bf)


def _in_proj(x, norm_g, w_in_bf16):
    b, s, d = x.shape
    nblk = s // TOK_BLOCK
    pad_blk = KV_PAD // TOK_BLOCK
    assert pad_blk * TOK_BLOCK == KV_PAD and nblk * TOK_BLOCK == s
    tok_idx = lambda bi, si: (bi, jnp.maximum(si - pad_blk, 0), 0)
    return pl.pallas_call(
        _in_proj_kernel,
        grid=(b, nblk + pad_blk),
        in_specs=[
            pl.BlockSpec((1, TOK_BLOCK, d), tok_idx),
            pl.BlockSpec((1, d), lambda bi, si: (0, 0)),
            pl.BlockSpec((d, D_IN_PROJ), lambda bi, si: (0, 0)),
        ],
        out_specs=[
            pl.BlockSpec((1, TOK_BLOCK, ZQ_W), tok_idx),
            pl.BlockSpec((1, TOK_BLOCK, ZKV_W), lambda bi, si: (bi, si, 0)),
        ],
        out_shape=[
            jax.ShapeDtypeStruct((b, s, ZQ_W), jnp.bfloat16),
            jax.ShapeDtypeStruct((b, s + KV_PAD, ZKV_W), jnp.bfloat16),
        ],
        compiler_params=pltpu.CompilerParams(
            dimension_semantics=("arbitrary", "arbitrary"),
            vmem_limit_bytes=VMEM_LIMIT),
        name="in_proj",
    )(x, norm_g, w_in_bf16)


def _build_bias_tables(b0_ref, bias_a_ref, bias_b_ref):
    def band_geometry(width, n_prev):
        i = lax.broadcasted_iota(jnp.int32, (PAIR, width), 0)
        jk = lax.broadcasted_iota(jnp.int32, (PAIR, width), 1)
        ci, kc = i // CHUNK, jk // CHUNK
        valid = (kc >= ci) & (kc <= ci + n_prev)
        return i, jk, valid

    i, jk, valid = band_geometry(A_BAND, A_PREV)
    dist = jnp.abs(i - jk + A_PREV * CHUNK).astype(jnp.float32)
    for h in range(A_HEADS):
        slope = 2.0 ** (-8.0 * (h + 1) / A_HEADS) * LOG2E
        base = jnp.where(valid, -slope * dist, NEG_BIG)
        rows = slice((h % A_GROUP) * PAIR, (h % A_GROUP + 1) * PAIR)
        for var in range(A_START_VARIANTS):
            first_valid = max(A_PREV * CHUNK - PAIR * var, 0)
            bias_a_ref[var, h // A_GROUP, rows, :] = jnp.where(jk < first_valid, NEG_BIG, base)

    i, jk, valid = band_geometry(B_BAND, B_PREV)
    for h in range(B_HEADS):
        row = jnp.broadcast_to(b0_ref[h:h + 1, :], (PAIR, TOEP_W))
        toep = pltpu.roll(row, 0, axis=1, stride=1, stride_axis=0)[:, :B_BAND]
        base = jnp.where(valid, toep * LOG2E, NEG_BIG)
        rows = slice((h % 2) * PAIR, (h % 2 + 1) * PAIR)
        for var in range(B_START_VARIANTS):
            first_valid = max(B_PREV * CHUNK - PAIR * var, 0)
            bias_b_ref[var, h // 2, rows, :] = jnp.where(jk < first_valid, NEG_BIG, base)


def _softmax_pv(s, v, sink):
    m = jnp.max(s, axis=-1, keepdims=True)
    if sink is not None:
        m = jnp.maximum(m, sink)
    e = jnp.exp2(s - m).astype(jnp.bfloat16)
    v_ones = jnp.concatenate([v, jnp.ones(v.shape, v.dtype)], axis=1)
    o = jnp.dot(e, v_ones, preferred_element_type=jnp.float32)
    denom = o[:, LANES:]
    if sink is not None:
        denom = denom + jnp.exp2(sink - m)
    return o[:, :LANES] / denom


def _silu(g):
    return g * (1.0 / (1.0 + jnp.exp(-g)))


def _attn_kernel(zq_ref, zkv_ref, b0_ref, sink_ref, y_ref, bias_a_ref, bias_b_ref):
    blk = pl.program_id(1)
    pairs_per_blk = TOK_BLOCK // PAIR

    @pl.when((pl.program_id(0) == 0) & (blk == 0))
    def _():
        _build_bias_tables(b0_ref, bias_a_ref, bias_b_ref)

    lane = lax.broadcasted_iota(jnp.int32, (PAIR, LANES), 1)
    lo = lane < HEAD_DIM
    zero = jnp.zeros((), jnp.bfloat16)

    def split_heads(qp):
        return jnp.concatenate([jnp.where(lo, qp, zero), jnp.where(lo, zero, qp)], axis=0)

    def merge_heads(o):
        return jnp.where(lo, o[0:PAIR], o[PAIR:2 * PAIR])

    def pair_body(pp, carry):
        row0 = pl.multiple_of(pp * PAIR, PAIR)
        gp = blk * pairs_per_blk + pp
        band0 = pl.multiple_of(gp * PAIR, PAIR)
        rows = pl.ds(row0, PAIR)

        a_rows = pl.ds(band0 + (KV_PAD - A_PREV * CHUNK), A_BAND)
        var_a = jnp.minimum(gp, A_START_VARIANTS - 1)
        for g in range(A_KV_HEADS):
            q0 = zq_ref[0, rows, 256 * g:256 * g + 128]
            q1 = zq_ref[0, rows, 256 * g + 128:256 * g + 256]
            lhs = jnp.concatenate([split_heads(q0), split_heads(q1)], axis=0)
            k = zkv_ref[0, a_rows, 128 * g:128 * g + 128]
            v = zkv_ref[0, a_rows, 256 + 128 * g:256 + 128 * g + 128]
            s = lax.dot_general(lhs, k, (((1,), (1,)), ((), ())),
                                preferred_element_type=jnp.float32)
            o = _softmax_pv(s + bias_a_ref[var_a, g], v, sink_ref[g])
            out = jnp.concatenate([merge_heads(o[0:2 * PAIR]), merge_heads(o[2 * PAIR:4 * PAIR])],
                                  axis=1)
            gate = zq_ref[0, rows, 512 + 256 * g:512 + 256 * g + 256].astype(jnp.float32)
            y_ref[0, rows, 256 * g:256 * g + 256] = (out * _silu(gate)).astype(y_ref.dtype)

        b_rows = pl.ds(band0, B_BAND)
        var_b = jnp.minimum(gp, B_START_VARIANTS - 1)
        for hp in range(B_HEADS // 2):
            qp = zq_ref[0, rows, 1024 + 128 * hp:1024 + 128 * hp + 128]
            k = zkv_ref[0, b_rows, 512 + 128 * hp:512 + 128 * hp + 128]
            v = zkv_ref[0, b_rows, 1024 + 128 * hp:1024 + 128 * hp + 128]
            s = lax.dot_general(split_heads(qp), k, (((1,), (1,)), ((), ())),
                                preferred_element_type=jnp.float32)
            out = merge_heads(_softmax_pv(s + bias_b_ref[var_b, hp], v, None))
            gate = zq_ref[0, rows, 1536 + 128 * hp:1536 + 128 * hp + 128].astype(jnp.float32)
            y_ref[0, rows, 512 + 128 * hp:512 + 128 * hp + 128] = (
                out * _silu(gate)).astype(y_ref.dtype)
        return carry

    lax.fori_loop(0, pairs_per_blk, pair_body, 0)


def _toeplitz_base_rows(rel_bias_b):
    tbl = rel_bias_b.astype(jnp.float32)
    h = tbl.shape[0]
    far = tbl[:, 2 * MAX_REL:2 * MAX_REL + 1]
    n_far = B_PREV * CHUNK - MAX_REL
    n_mid = B_BAND - n_far
    assert n_mid <= 2 * MAX_REL
    mid = tbl[:, 2 * MAX_REL:2 * MAX_REL - n_mid:-1]
    return jnp.concatenate(
        [jnp.broadcast_to(far, (h, n_far)), mid, jnp.broadcast_to(far, (h, TOEP_W - B_BAND))], axis=1)


def _attention(zq, zkv, b0, sink):
    b, s, _ = zq.shape
    nblk = s // TOK_BLOCK
    const3 = lambda bi, si: (0, 0, 0)
    const2 = lambda bi, si: (0, 0)
    return pl.pallas_call(
        _attn_kernel,
        grid=(b, nblk),
        in_specs=[
            pl.BlockSpec((1, TOK_BLOCK, ZQ_W), lambda bi, si: (bi, si, 0)),
            pl.BlockSpec((1, s + KV_PAD, ZKV_W), lambda bi, si: (bi, 0, 0)),
            pl.BlockSpec(b0.shape, const2),
            pl.BlockSpec(sink.shape, const3),
        ],
        out_specs=pl.BlockSpec((1, TOK_BLOCK, D_MODEL), lambda bi, si: (bi, si, 0)),
        out_shape=jax.ShapeDtypeStruct((b, s, D_MODEL), jnp.bfloat16),
        scratch_shapes=[
            pltpu.VMEM((A_START_VARIANTS, A_KV_HEADS, A_GROUP * PAIR, A_BAND), jnp.float32),
            pltpu.VMEM((B_START_VARIANTS, B_HEADS // 2, 2 * PAIR, B_BAND), jnp.float32),
        ],
        compiler_params=pltpu.CompilerParams(
            dimension_semantics=("arbitrary", "arbitrary"),
            vmem_limit_bytes=VMEM_LIMIT),
        name="attention",
    )(zq, zkv, b0, sink)


def _out_proj_kernel(x_ref, y_ref, p_ref, w_out_ref, ple_g_ref, w_gate_ref, w_proj_ref,
                     fin_g_ref, o_ref):
    h = x_ref[...] + jnp.dot(y_ref[...], w_out_ref[...], preferred_element_type=jnp.float32)
    u = (_rms_scale(h) * ple_g_ref[...]).astype(jnp.bfloat16)
    gl = jnp.dot(u, w_gate_ref[...], preferred_element_type=jnp.float32)
    gate = 1.0 / (1.0 + jnp.exp(-gl))
    pe = jnp.dot(p_ref[...].astype(jnp.bfloat16), w_proj_ref[...],
                 preferred_element_type=jnp.float32)
    h = h + pe * gate
    o_ref[...] = _rms_scale(h) * fin_g_ref[...]


def _out_proj(x2, y2, p2, w_out, ple_g, w_gate, w_proj, fin_g):
    n, d = x2.shape
    pd = p2.shape[1]
    tok = lambda i: (i, 0)
    const = lambda i: (0, 0)
    return pl.pallas_call(
        _out_proj_kernel,
        grid=(n // TOK_BLOCK,),
        in_specs=[
            pl.BlockSpec((TOK_BLOCK, d), tok),
            pl.BlockSpec((TOK_BLOCK, d), tok),
            pl.BlockSpec((TOK_BLOCK, pd), tok),
            pl.BlockSpec((d, d), const),
            pl.BlockSpec((1, d), const),
            pl.BlockSpec((d, d), const),
            pl.BlockSpec((pd, d), const),
            pl.BlockSpec((1, d), const),
        ],
        out_specs=pl.BlockSpec((TOK_BLOCK, d), tok),
        out_shape=jax.ShapeDtypeStruct((n, d), jnp.float32),
        compiler_params=pltpu.CompilerParams(
            dimension_semantics=("arbitrary",),
            vmem_limit_bytes=VMEM_LIMIT),
        name="out_proj",
    )(x2, y2, p2, w_out, ple_g, w_gate, w_proj, fin_g)


@jax.jit
def kernel(x, p, norm_g, w_in, sink_a, rel_bias_b, w_out, ple_norm_g, w_ple_proj, w_ple_gate,
           final_norm_g):
    b, s, d = x.shape
    bf = jnp.bfloat16
    assert norm_g.shape[0] == 1, "the final RMSNorm is fused into the (single) layer's output kernel"
    zq, zkv = _in_proj(x, norm_g[0][None], w_in[0].astype(bf))
    sink = jnp.repeat(sink_a[0].astype(jnp.float32) * LOG2E, PAIR)
    sink = sink.reshape(A_KV_HEADS, A_GROUP * PAIR, 1)
    y = _attention(zq, zkv, _toeplitz_base_rows(rel_bias_b[0]), sink)
    out = _out_proj(x.reshape(b * s, d), y.reshape(b * s, d), p[0].reshape(b * s, -1),
                    w_out[0].astype(bf), ple_norm_g[0][None], w_ple_gate[0].astype(bf),
                    w_ple_proj[0].astype(bf), final_norm_g[None])
    return out.reshape(b, s, d)
```

```python
import math

import jax
import jax.numpy as jnp
import numpy as np
from jax import lax
from jax.experimental import pallas as pl
from jax.experimental.pallas import tpu as pltpu

D_MODEL = 1024
CHUNK = 64
HEAD_DIM = 64
A_HEADS = 8
A_KV_HEADS = 2
A_GROUP = A_HEADS // A_KV_HEADS
B_HEADS = 8
A_PREV = 2
B_PREV = 8
MAX_REL = 128
RMS_EPS = 1e-6
NEG_BIG = -1e30
LOG2E = math.log2(math.e)

A_Q = A_HEADS * HEAD_DIM
A_KV = A_KV_HEADS * HEAD_DIM
A_W = 512
B_W = 512
OFF_QA, OFF_KA, OFF_VA, OFF_GA = 0, 512, 640, 768
OFF_QB, OFF_KB, OFF_VB, OFF_GB = 1280, 1792, 2304, 2816
D_IN_PROJ = 3328

LANES = 128
PAIR = 2 * CHUNK
A_BAND = (A_PREV + 2) * CHUNK
B_BAND = (B_PREV + 2) * CHUNK
KV_PAD = B_PREV * CHUNK
A_START_VARIANTS = A_PREV * CHUNK // PAIR + 1
B_START_VARIANTS = B_PREV * CHUNK // PAIR + 1
TOEP_W = B_BAND + PAIR
ZQ_W = 2048
ZKV_W = 1536
TOK_BLOCK = 512
ATTN_BLOCK = 1024
VMEM_LIMIT = 56 * 1024 * 1024
Q_SCALE = HEAD_DIM ** -0.5 * LOG2E


def _rms_scale(xf):
    var = jnp.mean(xf * xf, axis=-1, keepdims=True)
    return xf * lax.rsqrt(var + RMS_EPS)


def _dup_halves(v):
    rolled = pltpu.roll(v, HEAD_DIM, axis=1)
    lane = lax.broadcasted_iota(jnp.int32, v.shape, 1)
    lo = lane < HEAD_DIM
    return jnp.where(lo, v, rolled), jnp.where(lo, rolled, v)


def _in_proj_kernel(x_ref, g_ref, w_ref, zq_ref, zkv_ref):
    s = pl.program_id(1)

    @pl.when(s == 0)
    def _():
        zkv_ref[0] = jnp.zeros(zkv_ref.shape[1:], zkv_ref.dtype)

    @pl.when(s > 0)
    def _():
        u = (_rms_scale(x_ref[0]) * g_ref[...]).astype(jnp.bfloat16)

        def proj(off, width):
            return jnp.dot(u, w_ref[:, off:off + width], preferred_element_type=jnp.float32)

        bf = jnp.bfloat16
        zq_ref[0, :, 0:512] = (proj(OFF_QA, A_Q) * Q_SCALE).astype(bf)
        zq_ref[0, :, 512:1024] = proj(OFF_GA, A_W).astype(bf)
        zq_ref[0, :, 1024:1536] = (proj(OFF_QB, B_W) * Q_SCALE).astype(bf)
        zq_ref[0, :, 1536:2048] = proj(OFF_GB, B_W).astype(bf)
        kd0, kd1 = _dup_halves(proj(OFF_KA, A_KV))
        vd0, vd1 = _dup_halves(proj(OFF_VA, A_KV))
        zkv_ref[0, :, 0:128] = kd0.astype(bf)
        zkv_ref[0, :, 128:256] = kd1.astype(bf)
        zkv_ref[0, :, 256:384] = vd0.astype(bf)
        zkv_ref[0, :, 384:512] = vd1.astype(bf)
        zkv_ref[0, :, 512:1024] = proj(OFF_KB, B_W).astype(bf)
        zkv_ref[0, :, 1024:1536] = proj(OFF_VB, B_W).astype(bf)


def _in_proj(x, norm_g, w_in_bf16):
    b, s, d = x.shape
    nblk = s // TOK_BLOCK
    pad_blk = KV_PAD // TOK_BLOCK
    assert pad_blk * TOK_BLOCK == KV_PAD and nblk * TOK_BLOCK == s
    tok_idx = lambda bi, si: (bi, jnp.maximum(si - pad_blk, 0), 0)
    return pl.pallas_call(
        _in_proj_kernel,
        grid=(b, nblk + pad_blk),
        in_specs=[
            pl.BlockSpec((1, TOK_BLOCK, d), tok_idx),
            pl.BlockSpec((1, d), lambda bi, si: (0, 0)),
            pl.BlockSpec((d, D_IN_PROJ), lambda bi, si: (0, 0)),
        ],
        out_specs=[
            pl.BlockSpec((1, TOK_BLOCK, ZQ_W), tok_idx),
            pl.BlockSpec((1, TOK_BLOCK, ZKV_W), lambda bi, si: (bi, si, 0)),
        ],
        out_shape=[
            jax.ShapeDtypeStruct((b, s, ZQ_W), jnp.bfloat16),
            jax.ShapeDtypeStruct((b, s + KV_PAD, ZKV_W), jnp.bfloat16),
        ],
        compiler_params=pltpu.CompilerParams(
            dimension_semantics=("arbitrary", "arbitrary"),
            vmem_limit_bytes=VMEM_LIMIT),
        name="in_proj",
    )(x, norm_g, w_in_bf16)


def _build_bias_tables(b0_ref, bias_a_ref, bias_b_ref):
    def band_geometry(width, n_prev):
        i = lax.broadcasted_iota(jnp.int32, (PAIR, width), 0)
        jk = lax.broadcasted_iota(jnp.int32, (PAIR, width), 1)
        ci, kc = i // CHUNK, jk // CHUNK
        valid = (kc >= ci) & (kc <= ci + n_prev)
        return i, jk, valid

    i, jk, valid = band_geometry(A_BAND, A_PREV)
    dist = jnp.abs(i - jk + A_PREV * CHUNK).astype(jnp.float32)
    for h in range(A_HEADS):
        slope = 2.0 ** (-8.0 * (h + 1) / A_HEADS) * LOG2E
        base = jnp.where(valid, -slope * dist, NEG_BIG)
        rows = slice((h % A_GROUP) * PAIR, (h % A_GROUP + 1) * PAIR)
        for var in range(A_START_VARIANTS):
            first_valid = max(A_PREV * CHUNK - PAIR * var, 0)
            bias_a_ref[var, h // A_GROUP, rows, :] = jnp.where(jk < first_valid, NEG_BIG, base)

    i, jk, valid = band_geometry(B_BAND, B_PREV)
    for h in range(B_HEADS):
        row = jnp.broadcast_to(b0_ref[h:h + 1, :], (PAIR, TOEP_W))
        toep = pltpu.roll(row, 0, axis=1, stride=1, stride_axis=0)[:, :B_BAND]
        base = jnp.where(valid, toep * LOG2E, NEG_BIG)
        rows = slice((h % 2) * PAIR, (h % 2 + 1) * PAIR)
        for var in range(B_START_VARIANTS):
            first_valid = max(B_PREV * CHUNK - PAIR * var, 0)
            bias_b_ref[var, h // 2, rows, :] = jnp.where(jk < first_valid, NEG_BIG, base)


def _pv_with_row_sums(e, v):
    v_ones = jnp.concatenate([v, jnp.ones(v.shape, v.dtype)], axis=1)
    o = jnp.dot(e, v_ones, preferred_element_type=jnp.float32)
    return o[:, :LANES], o[:, LANES:]


def _software_pipeline(n, stages):
    first, second, third = stages
    assert n >= 4 and n % 2 == 0
    first(0, 0)
    first(1, 1)
    second(0, 0)

    def steady(i2, carry):
        i = 2 * i2
        first(i, 0)
        second(i - 1, 1)
        third(i - 2, 0)
        first(i + 1, 1)
        second(i, 0)
        third(i - 1, 1)
        return carry

    lax.fori_loop(1, n // 2, steady, 0)
    second(n - 1, 1)
    third(n - 2, 0)
    third(n - 1, 1)


def _silu(g):
    return g * (1.0 / (1.0 + jnp.exp(-g)))


def _attn_kernel(zq_ref, zkv_ref, b0_ref, sink_ref, y_ref, bias_a_ref, bias_b_ref,
                 sa_ref, pa_ref, ta_ref, sb_ref, pb_ref):
    blk = pl.program_id(1)
    pairs_per_blk = ATTN_BLOCK // PAIR

    @pl.when((pl.program_id(0) == 0) & (blk == 0))
    def _():
        _build_bias_tables(b0_ref, bias_a_ref, bias_b_ref)

    lane = lax.broadcasted_iota(jnp.int32, (PAIR, LANES), 1)
    lo = lane < HEAD_DIM
    zero = jnp.zeros((), jnp.bfloat16)

    def split_heads(qp):
        return jnp.concatenate([jnp.where(lo, qp, zero), jnp.where(lo, zero, qp)], axis=0)

    def merge_heads(o):
        return jnp.where(lo, o[0:PAIR], o[PAIR:2 * PAIR])

    nt_dims = (((1,), (1,)), ((), ()))

    def q_rows(pp):
        return pl.ds(pl.multiple_of(pp * PAIR, PAIR), PAIR)

    def band_rows(pp, n_prev, width):
        gp = blk * pairs_per_blk + pp
        return pl.ds(pl.multiple_of(gp * PAIR, PAIR) + (KV_PAD - n_prev * CHUNK), width)

    def start_variant(pp, n_variants):
        return jnp.minimum(blk * pairs_per_blk + pp, n_variants - 1)

    def unit_a(g):
        def scores(pp, slot):
            rows = q_rows(pp)
            q0 = zq_ref[0, rows, 256 * g:256 * g + 128]
            q1 = zq_ref[0, rows, 256 * g + 128:256 * g + 256]
            lhs = jnp.concatenate([split_heads(q0), split_heads(q1)], axis=0)
            k = zkv_ref[0, band_rows(pp, A_PREV, A_BAND), 128 * g:128 * g + 128]
            s = lax.dot_general(lhs, k, nt_dims, preferred_element_type=jnp.float32)
            sa_ref[slot] = s + bias_a_ref[start_variant(pp, A_START_VARIANTS), g]

        def weights(pp, slot):
            s = sa_ref[slot]
            sink = sink_ref[g]
            m = jnp.maximum(jnp.max(s, axis=-1, keepdims=True), sink)
            pa_ref[slot] = jnp.exp2(s - m).astype(jnp.bfloat16)
            ta_ref[slot] = jnp.exp2(sink - m)

        def output(pp, slot):
            rows = q_rows(pp)
            v = zkv_ref[0, band_rows(pp, A_PREV, A_BAND), 256 + 128 * g:256 + 128 * g + 128]
            o, denom = _pv_with_row_sums(pa_ref[slot], v)
            o = o / (denom + ta_ref[slot])
            out = jnp.concatenate([merge_heads(o[0:2 * PAIR]), merge_heads(o[2 * PAIR:4 * PAIR])],
                                  axis=1)
            gate = zq_ref[0, rows, 512 + 256 * g:512 + 256 * g + 256].astype(jnp.float32)
            y_ref[0, rows, 256 * g:256 * g + 256] = (out * _silu(gate)).astype(y_ref.dtype)

        return scores, weights, output

    def unit_b(hp):
        def scores(pp, slot):
            qp = zq_ref[0, q_rows(pp), 1024 + 128 * hp:1024 + 128 * hp + 128]
            k = zkv_ref[0, band_rows(pp, B_PREV, B_BAND), 512 + 128 * hp:512 + 128 * hp + 128]
            s = lax.dot_general(split_heads(qp), k, nt_dims, preferred_element_type=jnp.float32)
            sb_ref[slot] = s + bias_b_ref[start_variant(pp, B_START_VARIANTS), hp]

        def weights(pp, slot):
            s = sb_ref[slot]
            m = jnp.max(s, axis=-1, keepdims=True)
            pb_ref[slot] = jnp.exp2(s - m).astype(jnp.bfloat16)

        def output(pp, slot):
            rows = q_rows(pp)
            v = zkv_ref[0, band_rows(pp, B_PREV, B_BAND), 1024 + 128 * hp:1024 + 128 * hp + 128]
            o, denom = _pv_with_row_sums(pb_ref[slot], v)
            out = merge_heads(o / denom)
            gate = zq_ref[0, rows, 1536 + 128 * hp:1536 + 128 * hp + 128].astype(jnp.float32)
            y_ref[0, rows, 512 + 128 * hp:512 + 128 * hp + 128] = (
                out * _silu(gate)).astype(y_ref.dtype)

        return scores, weights, output

    for g in range(A_KV_HEADS):
        _software_pipeline(pairs_per_blk, unit_a(g))
    for hp in range(B_HEADS // 2):
        _software_pipeline(pairs_per_blk, unit_b(hp))


def _toeplitz_base_rows(rel_bias_b):
    tbl = rel_bias_b.astype(jnp.float32)
    h = tbl.shape[0]
    far = tbl[:, 2 * MAX_REL:2 * MAX_REL + 1]
    n_far = B_PREV * CHUNK - MAX_REL
    n_mid = B_BAND - n_far
    assert n_mid <= 2 * MAX_REL
    mid = tbl[:, 2 * MAX_REL:2 * MAX_REL - n_mid:-1]
    return jnp.concatenate(
        [jnp.broadcast_to(far, (h, n_far)), mid, jnp.broadcast_to(far, (h, TOEP_W - B_BAND))], axis=1)


def _attention(zq, zkv, b0, sink):
    b, s, _ = zq.shape
    nblk = s // ATTN_BLOCK
    const3 = lambda bi, si: (0, 0, 0)
    const2 = lambda bi, si: (0, 0)
    return pl.pallas_call(
        _attn_kernel,
        grid=(b, nblk),
        in_specs=[
            pl.BlockSpec((1, ATTN_BLOCK, ZQ_W), lambda bi, si: (bi, si, 0)),
            pl.BlockSpec((1, s + KV_PAD, ZKV_W), lambda bi, si: (bi, 0, 0)),
            pl.BlockSpec(b0.shape, const2),
            pl.BlockSpec(sink.shape, const3),
        ],
        out_specs=pl.BlockSpec((1, ATTN_BLOCK, D_MODEL), lambda bi, si: (bi, si, 0)),
        out_shape=jax.ShapeDtypeStruct((b, s, D_MODEL), jnp.bfloat16),
        scratch_shapes=[
            pltpu.VMEM((A_START_VARIANTS, A_KV_HEADS, A_GROUP * PAIR, A_BAND), jnp.float32),
            pltpu.VMEM((B_START_VARIANTS, B_HEADS // 2, 2 * PAIR, B_BAND), jnp.float32),
            pltpu.VMEM((2, A_GROUP * PAIR, A_BAND), jnp.float32),
            pltpu.VMEM((2, A_GROUP * PAIR, A_BAND), jnp.bfloat16),
            pltpu.VMEM((2, A_GROUP * PAIR, 1), jnp.float32),
            pltpu.VMEM((2, 2 * PAIR, B_BAND), jnp.float32),
            pltpu.VMEM((2, 2 * PAIR, B_BAND), jnp.bfloat16),
        ],
        compiler_params=pltpu.CompilerParams(
            dimension_semantics=("arbitrary", "arbitrary"),
            vmem_limit_bytes=VMEM_LIMIT),
        name="attention",
    )(zq, zkv, b0, sink)


def _out_proj_kernel(x_ref, y_ref, p_ref, w_out_ref, ple_g_ref, w_gate_ref, w_proj_ref,
                     fin_g_ref, o_ref):
    h = x_ref[...] + jnp.dot(y_ref[...], w_out_ref[...], preferred_element_type=jnp.float32)
    u = (_rms_scale(h) * ple_g_ref[...]).astype(jnp.bfloat16)
    gl = jnp.dot(u, w_gate_ref[...], preferred_element_type=jnp.float32)
    gate = 1.0 / (1.0 + jnp.exp(-gl))
    pe = jnp.dot(p_ref[...].astype(jnp.bfloat16), w_proj_ref[...],
                 preferred_element_type=jnp.float32)
    h = h + pe * gate
    o_ref[...] = _rms_scale(h) * fin_g_ref[...]


def _out_proj(x2, y2, p2, w_out, ple_g, w_gate, w_proj, fin_g):
    n, d = x2.shape
    pd = p2.shape[1]
    tok = lambda i: (i, 0)
    const = lambda i: (0, 0)
    return pl.pallas_call(
        _out_proj_kernel,
        grid=(n // TOK_BLOCK,),
        in_specs=[
            pl.BlockSpec((TOK_BLOCK, d), tok),
            pl.BlockSpec((TOK_BLOCK, d), tok),
            pl.BlockSpec((TOK_BLOCK, pd), tok),
            pl.BlockSpec((d, d), const),
            pl.BlockSpec((1, d), const),
            pl.BlockSpec((d, d), const),
            pl.BlockSpec((pd, d), const),
            pl.BlockSpec((1, d), const),
        ],
        out_specs=pl.BlockSpec((TOK_BLOCK, d), tok),
        out_shape=jax.ShapeDtypeStruct((n, d), jnp.float32),
        compiler_params=pltpu.CompilerParams(
            dimension_semantics=("arbitrary",),
            vmem_limit_bytes=VMEM_LIMIT),
        name="out_proj",
    )(x2, y2, p2, w_out, ple_g, w_gate, w_proj, fin_g)


@jax.jit
def kernel(x, p, norm_g, w_in, sink_a, rel_bias_b, w_out, ple_norm_g, w_ple_proj, w_ple_gate,
           final_norm_g):
    b, s, d = x.shape
    bf = jnp.bfloat16
    assert norm_g.shape[0] == 1, "the final RMSNorm is fused into the (single) layer's output kernel"
    zq, zkv = _in_proj(x, norm_g[0][None], w_in[0].astype(bf))
    sink = jnp.repeat(sink_a[0].astype(jnp.float32) * LOG2E, PAIR)
    sink = sink.reshape(A_KV_HEADS, A_GROUP * PAIR, 1)
    y = _attention(zq, zkv, _toeplitz_base_rows(rel_bias_b[0]), sink)
    out = _out_proj(x.reshape(b * s, d), y.reshape(b * s, d), p[0].reshape(b * s, -1),
                    w_out[0].astype(bf), ple_norm_g[0][None], w_ple_gate[0].astype(bf),
                    w_ple_proj[0].astype(bf), final_norm_g[None])
    return out.reshape(b, s, d)
```

```python
import math

import jax
import jax.numpy as jnp
import numpy as np
from jax import lax
from jax.experimental import pallas as pl
from jax.experimental.pallas import tpu as pltpu

D_MODEL = 1024
CHUNK = 64
HEAD_DIM = 64
A_HEADS = 8
A_KV_HEADS = 2
A_GROUP = A_HEADS // A_KV_HEADS
B_HEADS = 8
A_PREV = 2
B_PREV = 8
MAX_REL = 128
RMS_EPS = 1e-6
NEG_BIG = -1e30
LOG2E = math.log2(math.e)

A_Q = A_HEADS * HEAD_DIM
A_KV = A_KV_HEADS * HEAD_DIM
A_W = 512
B_W = 512
OFF_QA, OFF_KA, OFF_VA, OFF_GA = 0, 512, 640, 768
OFF_QB, OFF_KB, OFF_VB, OFF_GB = 1280, 1792, 2304, 2816
D_IN_PROJ = 3328

LANES = 128
PAIR = 2 * CHUNK
A_BAND = (A_PREV + 2) * CHUNK
B_BAND = (B_PREV + 2) * CHUNK
KV_PAD = B_PREV * CHUNK
TOEP_W = B_BAND + PAIR
ZQ_W = 2048
ZKV_W = 1536
TOK_BLOCK = 512
PIPE_UNROLL = 4
PIPE_LAG = 2
VMEM_LIMIT = 56 * 1024 * 1024
Q_SCALE = HEAD_DIM ** -0.5 * LOG2E


def _rms_scale(xf):
    var = jnp.mean(xf * xf, axis=-1, keepdims=True)
    return xf * lax.rsqrt(var + RMS_EPS)


def _dup_halves(v):
    rolled = pltpu.roll(v, HEAD_DIM, axis=1)
    lane = lax.broadcasted_iota(jnp.int32, v.shape, 1)
    lo = lane < HEAD_DIM
    return jnp.where(lo, v, rolled), jnp.where(lo, rolled, v)


def _in_proj_kernel(x_ref, g_ref, w_ref, zq_ref, zkv_ref):
    s = pl.program_id(1)

    @pl.when(s == 0)
    def _():
        zkv_ref[0] = jnp.zeros(zkv_ref.shape[1:], zkv_ref.dtype)

    @pl.when(s > 0)
    def _():
        u = (_rms_scale(x_ref[0]) * g_ref[...]).astype(jnp.bfloat16)

        def proj(off, width):
            return jnp.dot(u, w_ref[:, off:off + width], preferred_element_type=jnp.float32)

        bf = jnp.bfloat16
        zq_ref[0, :, 0:512] = (proj(OFF_QA, A_Q) * Q_SCALE).astype(bf)
        zq_ref[0, :, 512:1024] = proj(OFF_GA, A_W).astype(bf)
        zq_ref[0, :, 1024:1536] = (proj(OFF_QB, B_W) * Q_SCALE).astype(bf)
        zq_ref[0, :, 1536:2048] = proj(OFF_GB, B_W).astype(bf)
        kd0, kd1 = _dup_halves(proj(OFF_KA, A_KV))
        vd0, vd1 = _dup_halves(proj(OFF_VA, A_KV))
        zkv_ref[0, :, 0:128] = kd0.astype(bf)
        zkv_ref[0, :, 128:256] = kd1.astype(bf)
        zkv_ref[0, :, 256:384] = vd0.astype(bf)
        zkv_ref[0, :, 384:512] = vd1.astype(bf)
        zkv_ref[0, :, 512:1024] = proj(OFF_KB, B_W).astype(bf)
        zkv_ref[0, :, 1024:1536] = proj(OFF_VB, B_W).astype(bf)


def _in_proj(x, norm_g, w_in_bf16):
    b, s, d = x.shape
    nblk = s // TOK_BLOCK
    pad_blk = KV_PAD // TOK_BLOCK
    assert pad_blk * TOK_BLOCK == KV_PAD and nblk * TOK_BLOCK == s
    tok_idx = lambda bi, si: (bi, jnp.maximum(si - pad_blk, 0), 0)
    return pl.pallas_call(
        _in_proj_kernel,
        grid=(b, nblk + pad_blk),
        in_specs=[
            pl.BlockSpec((1, TOK_BLOCK, d), tok_idx),
            pl.BlockSpec((1, d), lambda bi, si: (0, 0)),
            pl.BlockSpec((d, D_IN_PROJ), lambda bi, si: (0, 0)),
        ],
        out_specs=[
            pl.BlockSpec((1, TOK_BLOCK, ZQ_W), tok_idx),
            pl.BlockSpec((1, TOK_BLOCK, ZKV_W), lambda bi, si: (bi, si, 0)),
        ],
        out_shape=[
            jax.ShapeDtypeStruct((b, s, ZQ_W), jnp.bfloat16),
            jax.ShapeDtypeStruct((b, s + KV_PAD, ZKV_W), jnp.bfloat16),
        ],
        compiler_params=pltpu.CompilerParams(
            dimension_semantics=("arbitrary", "arbitrary"),
            vmem_limit_bytes=VMEM_LIMIT),
        name="in_proj",
    )(x, norm_g, w_in_bf16)


def _build_bias_tables(b0_ref, bias_a_ref, bias_b_ref):
    def band_geometry(width, n_prev):
        i = lax.broadcasted_iota(jnp.int32, (PAIR, width), 0)
        jk = lax.broadcasted_iota(jnp.int32, (PAIR, width), 1)
        ci, kc = i // CHUNK, jk // CHUNK
        valid = (kc >= ci) & (kc <= ci + n_prev)
        return i, jk, valid

    i, jk, valid = band_geometry(A_BAND, A_PREV)
    dist = jnp.abs(i - jk + A_PREV * CHUNK).astype(jnp.float32)
    for h in range(A_HEADS):
        slope = 2.0 ** (-8.0 * (h + 1) / A_HEADS) * LOG2E
        rows = slice((h % A_GROUP) * PAIR, (h % A_GROUP + 1) * PAIR)
        bias_a_ref[h // A_GROUP, rows, :] = jnp.where(valid, -slope * dist, NEG_BIG)

    i, jk, valid = band_geometry(B_BAND, B_PREV)
    for h in range(B_HEADS):
        row = jnp.broadcast_to(b0_ref[h:h + 1, :], (PAIR, TOEP_W))
        toep = pltpu.roll(row, 0, axis=1, stride=1, stride_axis=0)[:, :B_BAND]
        rows = slice((h % 2) * PAIR, (h % 2 + 1) * PAIR)
        bias_b_ref[h // 2, rows, :] = jnp.where(valid, toep * LOG2E, NEG_BIG)


def _pv_with_row_sums(e, v):
    v_ones = jnp.concatenate([v, jnp.ones(v.shape, v.dtype)], axis=1)
    o = jnp.dot(e, v_ones, preferred_element_type=jnp.float32)
    return o[:, :LANES], o[:, LANES:]


def _software_pipeline(n, stages):
    first, second, third = stages
    u, lag = PIPE_UNROLL, PIPE_LAG
    assert n % u == 0 and n >= 2 * u and 2 * lag <= u

    for j in range(u):
        first(j, j)
        if j >= lag:
            second(j - lag, j - lag)
        if j >= 2 * lag:
            third(j - 2 * lag, j - 2 * lag)

    def steady(t, carry):
        for j in range(u):
            first(t * u + j, j)
            second(t * u + j - lag, (j - lag) % u)
            third(t * u + j - 2 * lag, (j - 2 * lag) % u)
        return carry

    lax.fori_loop(1, n // u, steady, 0)
    for j in range(u, u + 2 * lag):
        if j - lag < u:
            second(n - u + j - lag, (j - lag) % u)
        third(n - u + j - 2 * lag, (j - 2 * lag) % u)


def _silu(g):
    return g * (1.0 / (1.0 + jnp.exp(-g)))


def _attn_kernel(zq_ref, zkv_ref, b0_ref, sink_ref, y_ref, bias_a_ref, bias_b_ref,
                 sa_ref, pa_ref, ta_ref, sb_ref, pb_ref):
    n_pairs = zq_ref.shape[1] // PAIR

    @pl.when(pl.program_id(0) == 0)
    def _():
        _build_bias_tables(b0_ref, bias_a_ref, bias_b_ref)

    lane = lax.broadcasted_iota(jnp.int32, (PAIR, LANES), 1)
    lo = lane < HEAD_DIM
    zero = jnp.zeros((), jnp.bfloat16)
    nt_dims = (((1,), (1,)), ((), ()))

    def split_heads(qp):
        return jnp.concatenate([jnp.where(lo, qp, zero), jnp.where(lo, zero, qp)], axis=0)

    def merge_heads(o):
        return jnp.where(lo, o[0:PAIR], o[PAIR:2 * PAIR])

    def pair_start(pp):
        return pp * PAIR if isinstance(pp, int) else pl.multiple_of(pp * PAIR, PAIR)

    def q_rows(pp):
        return pl.ds(pair_start(pp), PAIR)

    def band_rows(pp, n_prev, width):
        return pl.ds(pair_start(pp) + (KV_PAD - n_prev * CHUNK), width)

    def mask_before_sequence(s, pp, n_prev):
        if not isinstance(pp, int) or pp * PAIR >= n_prev * CHUNK:
            return s
        jk = lax.broadcasted_iota(jnp.int32, s.shape, 1)
        return jnp.where(jk < n_prev * CHUNK - pp * PAIR, NEG_BIG, s)

    def unit_a(g):
        def scores(pp, slot):
            rows = q_rows(pp)
            q0 = zq_ref[0, rows, 256 * g:256 * g + 128]
            q1 = zq_ref[0, rows, 256 * g + 128:256 * g + 256]
            lhs = jnp.concatenate([split_heads(q0), split_heads(q1)], axis=0)
            k = zkv_ref[0, band_rows(pp, A_PREV, A_BAND), 128 * g:128 * g + 128]
            s = lax.dot_general(lhs, k, nt_dims, preferred_element_type=jnp.float32)
            sa_ref[slot] = mask_before_sequence(s + bias_a_ref[g], pp, A_PREV)

        def weights(pp, slot):
            sink = sink_ref[g]
            m = jnp.maximum(jnp.max(sa_ref[slot], axis=-1, keepdims=True), sink)
            for t in range(A_BAND // LANES):
                cols = slice(t * LANES, (t + 1) * LANES)
                pa_ref[slot, :, cols] = jnp.exp2(sa_ref[slot, :, cols] - m).astype(jnp.bfloat16)
            ta_ref[slot] = jnp.exp2(sink - m)

        def output(pp, slot):
            rows = q_rows(pp)
            v = zkv_ref[0, band_rows(pp, A_PREV, A_BAND), 256 + 128 * g:256 + 128 * g + 128]
            o, denom = _pv_with_row_sums(pa_ref[slot], v)
            o = o / (denom + ta_ref[slot])
            out = jnp.concatenate([merge_heads(o[0:2 * PAIR]), merge_heads(o[2 * PAIR:4 * PAIR])],
                                  axis=1)
            gate = zq_ref[0, rows, 512 + 256 * g:512 + 256 * g + 256].astype(jnp.float32)
            y_ref[0, rows, 256 * g:256 * g + 256] = (out * _silu(gate)).astype(y_ref.dtype)

        return scores, weights, output

    def unit_b(hp):
        def scores(pp, slot):
            qp = zq_ref[0, q_rows(pp), 1024 + 128 * hp:1024 + 128 * hp + 128]
            k = zkv_ref[0, band_rows(pp, B_PREV, B_BAND), 512 + 128 * hp:512 + 128 * hp + 128]
            s = lax.dot_general(split_heads(qp), k, nt_dims, preferred_element_type=jnp.float32)
            sb_ref[slot] = mask_before_sequence(s + bias_b_ref[hp], pp, B_PREV)

        def weights(pp, slot):
            s = sb_ref[slot]
            m = jnp.max(s, axis=-1, keepdims=True)
            pb_ref[slot] = jnp.exp2(s - m).astype(jnp.bfloat16)

        def output(pp, slot):
            rows = q_rows(pp)
            v = zkv_ref[0, band_rows(pp, B_PREV, B_BAND), 1024 + 128 * hp:1024 + 128 * hp + 128]
            o, denom = _pv_with_row_sums(pb_ref[slot], v)
            out = merge_heads(o / denom)
            gate = zq_ref[0, rows, 1536 + 128 * hp:1536 + 128 * hp + 128].astype(jnp.float32)
            y_ref[0, rows, 512 + 128 * hp:512 + 128 * hp + 128] = (
                out * _silu(gate)).astype(y_ref.dtype)

        return scores, weights, output

    for g in range(A_KV_HEADS):
        _software_pipeline(n_pairs, unit_a(g))
    for hp in range(B_HEADS // 2):
        _software_pipeline(n_pairs, unit_b(hp))


def _toeplitz_base_rows(rel_bias_b):
    tbl = rel_bias_b.astype(jnp.float32)
    h = tbl.shape[0]
    far = tbl[:, 2 * MAX_REL:2 * MAX_REL + 1]
    n_far = B_PREV * CHUNK - MAX_REL
    n_mid = B_BAND - n_far
    assert n_mid <= 2 * MAX_REL
    mid = tbl[:, 2 * MAX_REL:2 * MAX_REL - n_mid:-1]
    return jnp.concatenate(
        [jnp.broadcast_to(far, (h, n_far)), mid, jnp.broadcast_to(far, (h, TOEP_W - B_BAND))], axis=1)


def _attention(zq, zkv, b0, sink):
    b, s, _ = zq.shape
    const3 = lambda bi: (0, 0, 0)
    const2 = lambda bi: (0, 0)
    slots = PIPE_UNROLL
    return pl.pallas_call(
        _attn_kernel,
        grid=(b,),
        in_specs=[
            pl.BlockSpec((1, s, ZQ_W), lambda bi: (bi, 0, 0)),
            pl.BlockSpec((1, s + KV_PAD, ZKV_W), lambda bi: (bi, 0, 0)),
            pl.BlockSpec(b0.shape, const2),
            pl.BlockSpec(sink.shape, const3),
        ],
        out_specs=pl.BlockSpec((1, s, D_MODEL), lambda bi: (bi, 0, 0)),
        out_shape=jax.ShapeDtypeStruct((b, s, D_MODEL), jnp.bfloat16),
        scratch_shapes=[
            pltpu.VMEM((A_KV_HEADS, A_GROUP * PAIR, A_BAND), jnp.float32),
            pltpu.VMEM((B_HEADS // 2, 2 * PAIR, B_BAND), jnp.float32),
            pltpu.VMEM((slots, A_GROUP * PAIR, A_BAND), jnp.float32),
            pltpu.VMEM((slots, A_GROUP * PAIR, A_BAND), jnp.bfloat16),
            pltpu.VMEM((slots, A_GROUP * PAIR, LANES), jnp.float32),
            pltpu.VMEM((slots, 2 * PAIR, B_BAND), jnp.float32),
            pltpu.VMEM((slots, 2 * PAIR, B_BAND), jnp.bfloat16),
        ],
        compiler_params=pltpu.CompilerParams(
            dimension_semantics=("arbitrary",),
            vmem_limit_bytes=VMEM_LIMIT),
        name="attention",
    )(zq, zkv, b0, sink)


def _out_proj_kernel(x_ref, y_ref, p_ref, w_out_ref, ple_g_ref, w_gate_ref, w_proj_ref,
                     fin_g_ref, o_ref):
    h = x_ref[...] + jnp.dot(y_ref[...], w_out_ref[...], preferred_element_type=jnp.float32)
    u = (_rms_scale(h) * ple_g_ref[...]).astype(jnp.bfloat16)
    gl = jnp.dot(u, w_gate_ref[...], preferred_element_type=jnp.float32)
    gate = 1.0 / (1.0 + jnp.exp(-gl))
    pe = jnp.dot(p_ref[...].astype(jnp.bfloat16), w_proj_ref[...],
                 preferred_element_type=jnp.float32)
    h = h + pe * gate
    o_ref[...] = _rms_scale(h) * fin_g_ref[...]


def _out_proj(x2, y2, p2, w_out, ple_g, w_gate, w_proj, fin_g):
    n, d = x2.shape
    pd = p2.shape[1]
    tok = lambda i: (i, 0)
    const = lambda i: (0, 0)
    return pl.pallas_call(
        _out_proj_kernel,
        grid=(n // TOK_BLOCK,),
        in_specs=[
            pl.BlockSpec((TOK_BLOCK, d), tok),
            pl.BlockSpec((TOK_BLOCK, d), tok),
            pl.BlockSpec((TOK_BLOCK, pd), tok),
            pl.BlockSpec((d, d), const),
            pl.BlockSpec((1, d), const),
            pl.BlockSpec((d, d), const),
            pl.BlockSpec((pd, d), const),
            pl.BlockSpec((1, d), const),
        ],
        out_specs=pl.BlockSpec((TOK_BLOCK, d), tok),
        out_shape=jax.ShapeDtypeStruct((n, d), jnp.float32),
        compiler_params=pltpu.CompilerParams(
            dimension_semantics=("arbitrary",),
            vmem_limit_bytes=VMEM_LIMIT),
        name="out_proj",
    )(x2, y2, p2, w_out, ple_g, w_gate, w_proj, fin_g)


@jax.jit
def kernel(x, p, norm_g, w_in, sink_a, rel_bias_b, w_out, ple_norm_g, w_ple_proj, w_ple_gate,
           final_norm_g):
    b, s, d = x.shape
    bf = jnp.bfloat16
    assert norm_g.shape[0] == 1, "the final RMSNorm is fused into the (single) layer's output kernel"
    zq, zkv = _in_proj(x, norm_g[0][None], w_in[0].astype(bf))
    sink = jnp.repeat(sink_a[0].astype(jnp.float32) * LOG2E, PAIR)
    sink = jnp.broadcast_to(sink.reshape(A_KV_HEADS, A_GROUP * PAIR, 1),
                            (A_KV_HEADS, A_GROUP * PAIR, LANES))
    y = _attention(zq, zkv, _toeplitz_base_rows(rel_bias_b[0]), sink)
    out = _out_proj(x.reshape(b * s, d), y.reshape(b * s, d), p[0].reshape(b * s, -1),
                    w_out[0].astype(bf), ple_norm_g[0][None], w_ple_gate[0].astype(bf),
                    w_ple_proj[0].astype(bf), final_norm_g[None])
    return out.reshape(b, s, d)
```

```python
import math

import jax
import jax.numpy as jnp
from jax import lax
from jax.experimental import pallas as pl
from jax.experimental.pallas import tpu as pltpu

D_MODEL = 1024
CHUNK = 64
HEAD_DIM = 64
A_HEADS = 8
A_KV_HEADS = 2
A_GROUP = A_HEADS // A_KV_HEADS
B_HEADS = 8
A_PREV = 2
B_PREV = 8
MAX_REL = 128
RMS_EPS = 1e-6
NEG_BIG = -1e30
LOG2E = math.log2(math.e)

A_Q = A_HEADS * HEAD_DIM
A_KV = A_KV_HEADS * HEAD_DIM
A_W = 512
B_W = 512
OFF_QA, OFF_KA, OFF_VA, OFF_GA = 0, 512, 640, 768
OFF_QB, OFF_KB, OFF_VB, OFF_GB = 1280, 1792, 2304, 2816
D_IN_PROJ = 3328

LANES = 128
PAIR = 2 * CHUNK
A_BAND = (A_PREV + 2) * CHUNK
B_BAND = (B_PREV + 2) * CHUNK
KV_PAD = B_PREV * CHUNK
TOEP_W = B_BAND + PAIR
ZQ_W = 1024
ZKVA_W = 512
ZKVB_W = 1024
TOK_BLOCK = 512
PIPE_UNROLL = 4
PIPE_LAG = 2
VMEM_LIMIT = 56 * 1024 * 1024
Q_SCALE = HEAD_DIM ** -0.5 * LOG2E
NT_DIMS = (((1,), (1,)), ((), ()))


def _rms_scale(xf):
    var = jnp.mean(xf * xf, axis=-1, keepdims=True)
    return xf * lax.rsqrt(var + RMS_EPS)


def _silu(g):
    return g * (1.0 / (1.0 + jnp.exp(-g)))


def _dup_halves(v):
    rolled = pltpu.roll(v, HEAD_DIM, axis=1)
    lane = lax.broadcasted_iota(jnp.int32, v.shape, 1)
    lo = lane < HEAD_DIM
    return jnp.where(lo, v, rolled), jnp.where(lo, rolled, v)


def _in_proj_kernel(x_ref, g_ref, w_ref, zqa_ref, zqb_ref, zkva_ref, zkvb_ref):
    s = pl.program_id(1)

    @pl.when(s == 0)
    def _():
        zkva_ref[0] = jnp.zeros(zkva_ref.shape[1:], zkva_ref.dtype)
        zkvb_ref[0] = jnp.zeros(zkvb_ref.shape[1:], zkvb_ref.dtype)

    @pl.when(s > 0)
    def _():
        u = (_rms_scale(x_ref[0]) * g_ref[...]).astype(jnp.bfloat16)

        def proj(off, width):
            return jnp.dot(u, w_ref[:, off:off + width], preferred_element_type=jnp.float32)

        bf = jnp.bfloat16
        zqa_ref[0, :, 0:512] = (proj(OFF_QA, A_Q) * Q_SCALE).astype(bf)
        zqa_ref[0, :, 512:1024] = _silu(proj(OFF_GA, A_W)).astype(bf)
        zqb_ref[0, :, 0:512] = (proj(OFF_QB, B_W) * Q_SCALE).astype(bf)
        zqb_ref[0, :, 512:1024] = _silu(proj(OFF_GB, B_W)).astype(bf)
        kd0, kd1 = _dup_halves(proj(OFF_KA, A_KV))
        vd0, vd1 = _dup_halves(proj(OFF_VA, A_KV))
        zkva_ref[0, :, 0:128] = kd0.astype(bf)
        zkva_ref[0, :, 128:256] = kd1.astype(bf)
        zkva_ref[0, :, 256:384] = vd0.astype(bf)
        zkva_ref[0, :, 384:512] = vd1.astype(bf)
        zkvb_ref[0, :, 0:512] = proj(OFF_KB, B_W).astype(bf)
        zkvb_ref[0, :, 512:1024] = proj(OFF_VB, B_W).astype(bf)


def _in_proj(x, norm_g, w_in_bf16):
    b, s, d = x.shape
    nblk = s // TOK_BLOCK
    pad_blk = KV_PAD // TOK_BLOCK
    assert pad_blk * TOK_BLOCK == KV_PAD and nblk * TOK_BLOCK == s
    tok_idx = lambda bi, si: (bi, jnp.maximum(si - pad_blk, 0), 0)
    pad_idx = lambda bi, si: (bi, si, 0)
    bf = jnp.bfloat16
    return pl.pallas_call(
        _in_proj_kernel,
        grid=(b, nblk + pad_blk),
        in_specs=[
            pl.BlockSpec((1, TOK_BLOCK, d), tok_idx),
            pl.BlockSpec((1, d), lambda bi, si: (0, 0)),
            pl.BlockSpec((d, D_IN_PROJ), lambda bi, si: (0, 0)),
        ],
        out_specs=[
            pl.BlockSpec((1, TOK_BLOCK, ZQ_W), tok_idx),
            pl.BlockSpec((1, TOK_BLOCK, ZQ_W), tok_idx),
            pl.BlockSpec((1, TOK_BLOCK, ZKVA_W), pad_idx),
            pl.BlockSpec((1, TOK_BLOCK, ZKVB_W), pad_idx),
        ],
        out_shape=[
            jax.ShapeDtypeStruct((b, s, ZQ_W), bf),
            jax.ShapeDtypeStruct((b, s, ZQ_W), bf),
            jax.ShapeDtypeStruct((b, s + KV_PAD, ZKVA_W), bf),
            jax.ShapeDtypeStruct((b, s + KV_PAD, ZKVB_W), bf),
        ],
        compiler_params=pltpu.CompilerParams(
            dimension_semantics=("arbitrary", "arbitrary"),
            vmem_limit_bytes=VMEM_LIMIT),
        name="in_proj",
    )(x, norm_g, w_in_bf16)


def _band_geometry(width, n_prev):
    i = lax.broadcasted_iota(jnp.int32, (PAIR, width), 0)
    jk = lax.broadcasted_iota(jnp.int32, (PAIR, width), 1)
    ci, kc = i // CHUNK, jk // CHUNK
    return i, jk, (kc >= ci) & (kc <= ci + n_prev)


def _pv_with_row_sums(e, v):
    v_ones = jnp.concatenate([v, jnp.ones(v.shape, v.dtype)], axis=1)
    o = jnp.dot(e, v_ones, preferred_element_type=jnp.float32)
    return o[:, :LANES], o[:, LANES:]


def _software_pipeline(n, units):
    u, lag = PIPE_UNROLL, PIPE_LAG
    assert n % u == 0 and n >= 2 * u and 2 * lag <= u

    def step(base, j, stage_range):
        for first, second, third in units:
            if 0 in stage_range:
                first(base + j, j % u)
            if 1 in stage_range:
                second(base + j - lag, (j - lag) % u)
            if 2 in stage_range:
                third(base + j - 2 * lag, (j - 2 * lag) % u)

    for j in range(u):
        step(0, j, [k for k in range(3) if j >= k * lag])

    def steady(t, carry):
        for j in range(u):
            step(t * u, j, range(3))
        return carry

    lax.fori_loop(1, n // u, steady, 0)
    for j in range(u, u + 2 * lag):
        step(n - u, j, [k for k in range(1, 3) if j - k * lag < u])


def _head_masks():
    lane = lax.broadcasted_iota(jnp.int32, (PAIR, LANES), 1)
    return lane < HEAD_DIM


def _split_heads(qp, lo):
    zero = jnp.zeros((), qp.dtype)
    return jnp.concatenate([jnp.where(lo, qp, zero), jnp.where(lo, zero, qp)], axis=0)


def _merge_heads(o, lo):
    return jnp.where(lo, o[0:PAIR], o[PAIR:2 * PAIR])


def _pair_start(pp):
    return pp * PAIR if isinstance(pp, int) else pl.multiple_of(pp * PAIR, PAIR)


def _q_rows(pp):
    return pl.ds(_pair_start(pp), PAIR)


def _band_rows(pp, n_prev, width):
    return pl.ds(_pair_start(pp) + (KV_PAD - n_prev * CHUNK), width)


def _mask_before_sequence(s, pp, n_prev):
    if not isinstance(pp, int) or pp * PAIR >= n_prev * CHUNK:
        return s
    jk = lax.broadcasted_iota(jnp.int32, s.shape, 1)
    return jnp.where(jk < n_prev * CHUNK - pp * PAIR, NEG_BIG, s)


def _attn_a_kernel(zq_ref, zkv_ref, sink_ref, y_ref, bias_ref, s_ref, p_ref, t_ref):
    n_pairs = zq_ref.shape[1] // PAIR

    @pl.when(pl.program_id(0) == 0)
    def _():
        i, jk, valid = _band_geometry(A_BAND, A_PREV)
        dist = jnp.abs(i - jk + A_PREV * CHUNK).astype(jnp.float32)
        for h in range(A_HEADS):
            slope = 2.0 ** (-8.0 * (h + 1) / A_HEADS) * LOG2E
            rows = slice((h % A_GROUP) * PAIR, (h % A_GROUP + 1) * PAIR)
            bias_ref[h // A_GROUP, rows, :] = jnp.where(valid, -slope * dist, NEG_BIG)

    lo = _head_masks()

    def unit(g):
        def scores(pp, slot):
            rows = _q_rows(pp)
            q0 = zq_ref[0, rows, 256 * g:256 * g + 128]
            q1 = zq_ref[0, rows, 256 * g + 128:256 * g + 256]
            lhs = jnp.concatenate([_split_heads(q0, lo), _split_heads(q1, lo)], axis=0)
            k = zkv_ref[0, _band_rows(pp, A_PREV, A_BAND), 128 * g:128 * g + 128]
            s = lax.dot_general(lhs, k, NT_DIMS, preferred_element_type=jnp.float32)
            s_ref[g, slot] = _mask_before_sequence(s + bias_ref[g], pp, A_PREV)

        def weights(pp, slot):
            m = jnp.max(s_ref[g, slot], axis=-1, keepdims=True)
            m = jnp.broadcast_to(m, (A_GROUP * PAIR, LANES))
            for t in range(A_BAND // LANES):
                cols = slice(t * LANES, (t + 1) * LANES)
                p_ref[g, slot, :, cols] = jnp.exp2(s_ref[g, slot, :, cols] - m).astype(jnp.bfloat16)
            t_ref[g, slot] = jnp.exp2(sink_ref[g] - m)

        def output(pp, slot):
            rows = _q_rows(pp)
            v = zkv_ref[0, _band_rows(pp, A_PREV, A_BAND), 256 + 128 * g:256 + 128 * g + 128]
            o, denom = _pv_with_row_sums(p_ref[g, slot], v)
            denom = denom + t_ref[g, slot]
            o = jnp.concatenate([_merge_heads(o[0:2 * PAIR], lo),
                                 _merge_heads(o[2 * PAIR:4 * PAIR], lo)], axis=1)
            denom = jnp.concatenate([_merge_heads(denom[0:2 * PAIR], lo),
                                     _merge_heads(denom[2 * PAIR:4 * PAIR], lo)], axis=1)
            gate = zq_ref[0, rows, 512 + 256 * g:512 + 256 * g + 256].astype(jnp.float32)
            y_ref[0, rows, 256 * g:256 * g + 256] = (o / denom * gate).astype(y_ref.dtype)

        return scores, weights, output

    _software_pipeline(n_pairs, [unit(g) for g in range(A_KV_HEADS)])


def _attention_a(zq, zkv, sink):
    b, s, _ = zq.shape
    slots = PIPE_UNROLL
    rows = A_GROUP * PAIR
    return pl.pallas_call(
        _attn_a_kernel,
        grid=(b,),
        in_specs=[
            pl.BlockSpec((1, s, ZQ_W), lambda bi: (bi, 0, 0)),
            pl.BlockSpec((1, s + KV_PAD, ZKVA_W), lambda bi: (bi, 0, 0)),
            pl.BlockSpec(sink.shape, lambda bi: (0, 0, 0)),
        ],
        out_specs=pl.BlockSpec((1, s, A_W), lambda bi: (bi, 0, 0)),
        out_shape=jax.ShapeDtypeStruct((b, s, A_W), jnp.bfloat16),
        scratch_shapes=[
            pltpu.VMEM((A_KV_HEADS, rows, A_BAND), jnp.float32),
            pltpu.VMEM((A_KV_HEADS, slots, rows, A_BAND), jnp.float32),
            pltpu.VMEM((A_KV_HEADS, slots, rows, A_BAND), jnp.bfloat16),
            pltpu.VMEM((A_KV_HEADS, slots, rows, LANES), jnp.float32),
        ],
        compiler_params=pltpu.CompilerParams(
            dimension_semantics=("arbitrary",),
            vmem_limit_bytes=VMEM_LIMIT),
        name="attention_a",
    )(zq, zkv, sink)


def _attn_b_kernel(zq_ref, zkv_ref, b0_ref, y_ref, bias_ref, s_ref, p_ref):
    n_pairs = zq_ref.shape[1] // PAIR

    @pl.when(pl.program_id(0) == 0)
    def _():
        _, _, valid = _band_geometry(B_BAND, B_PREV)
        for h in range(B_HEADS):
            row = jnp.broadcast_to(b0_ref[h:h + 1, :], (PAIR, TOEP_W))
            toep = pltpu.roll(row, 0, axis=1, stride=1, stride_axis=0)[:, :B_BAND]
            rows = slice((h % 2) * PAIR, (h % 2 + 1) * PAIR)
            bias_ref[h // 2, rows, :] = jnp.where(valid, toep * LOG2E, NEG_BIG)

    lo = _head_masks()

    def unit(hp):
        def scores(pp, slot):
            qp = zq_ref[0, _q_rows(pp), 128 * hp:128 * hp + 128]
            k = zkv_ref[0, _band_rows(pp, B_PREV, B_BAND), 128 * hp:128 * hp + 128]
            s = lax.dot_general(_split_heads(qp, lo), k, NT_DIMS,
                                preferred_element_type=jnp.float32)
            s_ref[hp, slot] = _mask_before_sequence(s + bias_ref[hp], pp, B_PREV)

        def weights(pp, slot):
            s = s_ref[hp, slot]
            m = jnp.max(s, axis=-1, keepdims=True)
            p_ref[hp, slot] = jnp.exp2(s - m).astype(jnp.bfloat16)

        def output(pp, slot):
            rows = _q_rows(pp)
            v = zkv_ref[0, _band_rows(pp, B_PREV, B_BAND), 512 + 128 * hp:512 + 128 * hp + 128]
            o, denom = _pv_with_row_sums(p_ref[hp, slot], v)
            gate = zq_ref[0, rows, 512 + 128 * hp:512 + 128 * hp + 128].astype(jnp.float32)
            y_ref[0, rows, 128 * hp:128 * hp + 128] = (
                _merge_heads(o, lo) / _merge_heads(denom, lo) * gate).astype(y_ref.dtype)

        return scores, weights, output

    _software_pipeline(n_pairs, [unit(hp) for hp in range(B_HEADS // 2)])


def _toeplitz_base_rows(rel_bias_b):
    tbl = rel_bias_b.astype(jnp.float32)
    h = tbl.shape[0]
    far = tbl[:, 2 * MAX_REL:2 * MAX_REL + 1]
    n_far = B_PREV * CHUNK - MAX_REL
    n_mid = B_BAND - n_far
    assert n_mid <= 2 * MAX_REL
    mid = tbl[:, 2 * MAX_REL:2 * MAX_REL - n_mid:-1]
    return jnp.concatenate(
        [jnp.broadcast_to(far, (h, n_far)), mid, jnp.broadcast_to(far, (h, TOEP_W - B_BAND))], axis=1)


def _attention_b(zq, zkv, b0):
    b, s, _ = zq.shape
    slots = PIPE_UNROLL
    units = B_HEADS // 2
    return pl.pallas_call(
        _attn_b_kernel,
        grid=(b,),
        in_specs=[
            pl.BlockSpec((1, s, ZQ_W), lambda bi: (bi, 0, 0)),
            pl.BlockSpec((1, s + KV_PAD, ZKVB_W), lambda bi: (bi, 0, 0)),
            pl.BlockSpec(b0.shape, lambda bi: (0, 0)),
        ],
        out_specs=pl.BlockSpec((1, s, B_W), lambda bi: (bi, 0, 0)),
        out_shape=jax.ShapeDtypeStruct((b, s, B_W), jnp.bfloat16),
        scratch_shapes=[
            pltpu.VMEM((units, 2 * PAIR, B_BAND), jnp.float32),
            pltpu.VMEM((units, slots, 2 * PAIR, B_BAND), jnp.float32),
            pltpu.VMEM((units, slots, 2 * PAIR, B_BAND), jnp.bfloat16),
        ],
        compiler_params=pltpu.CompilerParams(
            dimension_semantics=("arbitrary",),
            vmem_limit_bytes=VMEM_LIMIT),
        name="attention_b",
    )(zq, zkv, b0)


def _out_proj_kernel(x_ref, ya_ref, yb_ref, p_ref, w_out_ref, ple_g_ref, w_gate_ref, w_proj_ref,
                     fin_g_ref, o_ref):
    y = jnp.concatenate([ya_ref[...], yb_ref[...]], axis=1)
    h = x_ref[...] + jnp.dot(y, w_out_ref[...], preferred_element_type=jnp.float32)
    u = (_rms_scale(h) * ple_g_ref[...]).astype(jnp.bfloat16)
    gl = jnp.dot(u, w_gate_ref[...], preferred_element_type=jnp.float32)
    gate = 1.0 / (1.0 + jnp.exp(-gl))
    pe = jnp.dot(p_ref[...].astype(jnp.bfloat16), w_proj_ref[...],
                 preferred_element_type=jnp.float32)
    h = h + pe * gate
    o_ref[...] = _rms_scale(h) * fin_g_ref[...]


def _out_proj(x2, ya2, yb2, p2, w_out, ple_g, w_gate, w_proj, fin_g):
    n, d = x2.shape
    pd = p2.shape[1]
    tok = lambda i: (i, 0)
    const = lambda i: (0, 0)
    return pl.pallas_call(
        _out_proj_kernel,
        grid=(n // TOK_BLOCK,),
        in_specs=[
            pl.BlockSpec((TOK_BLOCK, d), tok),
            pl.BlockSpec((TOK_BLOCK, ya2.shape[1]), tok),
            pl.BlockSpec((TOK_BLOCK, yb2.shape[1]), tok),
            pl.BlockSpec((TOK_BLOCK, pd), tok),
            pl.BlockSpec((d, d), const),
            pl.BlockSpec((1, d), const),
            pl.BlockSpec((d, d), const),
            pl.BlockSpec((pd, d), const),
            pl.BlockSpec((1, d), const),
        ],
        out_specs=pl.BlockSpec((TOK_BLOCK, d), tok),
        out_shape=jax.ShapeDtypeStruct((n, d), jnp.float32),
        compiler_params=pltpu.CompilerParams(
            dimension_semantics=("arbitrary",),
            vmem_limit_bytes=VMEM_LIMIT),
        name="out_proj",
    )(x2, ya2, yb2, p2, w_out, ple_g, w_gate, w_proj, fin_g)


@jax.jit
def kernel(x, p, norm_g, w_in, sink_a, rel_bias_b, w_out, ple_norm_g, w_ple_proj, w_ple_gate,
           final_norm_g):
    b, s, d = x.shape
    bf = jnp.bfloat16
    assert norm_g.shape[0] == 1, "the final RMSNorm is fused into the (single) layer's output kernel"
    zqa, zqb, zkva, zkvb = _in_proj(x, norm_g[0][None], w_in[0].astype(bf))
    sink = jnp.repeat(sink_a[0].astype(jnp.float32) * LOG2E, PAIR)
    sink = jnp.broadcast_to(sink.reshape(A_KV_HEADS, A_GROUP * PAIR, 1),
                            (A_KV_HEADS, A_GROUP * PAIR, LANES))
    ya = _attention_a(zqa, zkva, sink)
    yb = _attention_b(zqb, zkvb, _toeplitz_base_rows(rel_bias_b[0]))
    out = _out_proj(x.reshape(b * s, d), ya.reshape(b * s, -1), yb.reshape(b * s, -1),
                    p[0].reshape(b * s, -1), w_out[0].astype(bf), ple_norm_g[0][None],
                    w_ple_gate[0].astype(bf), w_ple_proj[0].astype(bf), final_norm_g[None])
    return out.reshape(b, s, d)
```

```python
import math

import jax
import jax.numpy as jnp
from jax import lax
from jax.experimental import pallas as pl
from jax.experimental.pallas import tpu as pltpu

D_MODEL = 1024
CHUNK = 64
HEAD_DIM = 64
A_HEADS = 8
A_KV_HEADS = 2
A_GROUP = A_HEADS // A_KV_HEADS
B_HEADS = 8
A_PREV = 2
B_PREV = 8
MAX_REL = 128
RMS_EPS = 1e-6
NEG_BIG = -1e30
LOG2E = math.log2(math.e)

A_Q = A_HEADS * HEAD_DIM
A_KV = A_KV_HEADS * HEAD_DIM
A_W = 512
B_W = 512
OFF_QA, OFF_KA, OFF_VA, OFF_GA = 0, 512, 640, 768
OFF_QB, OFF_KB, OFF_VB, OFF_GB = 1280, 1792, 2304, 2816
D_IN_PROJ = 3328

LANES = 128
PAIR = 2 * CHUNK
A_BAND = (A_PREV + 2) * CHUNK
B_BAND = (B_PREV + 2) * CHUNK
TOK_BLOCK = 1024
IN_SUB_BLOCK = 256
OUT_SUB_BLOCK = 512
KV_PAD = TOK_BLOCK
assert KV_PAD >= B_PREV * CHUNK
TOEP_W = B_BAND + PAIR
ZQ_W = 1024
ZKVA_W = 512
ZKVB_W = 1024
PIPE_UNROLL = 4
PIPE_LAG = 2
VMEM_LIMIT = 56 * 1024 * 1024
Q_SCALE = HEAD_DIM ** -0.5 * LOG2E
NT_DIMS = (((1,), (1,)), ((), ()))


def _rms_scale(xf):
    var = jnp.mean(xf * xf, axis=-1, keepdims=True)
    return xf * lax.rsqrt(var + RMS_EPS)


def _silu(g):
    return g * (1.0 / (1.0 + jnp.exp(-g)))


def _dup_halves(v):
    rolled = pltpu.roll(v, HEAD_DIM, axis=1)
    lane = lax.broadcasted_iota(jnp.int32, v.shape, 1)
    lo = lane < HEAD_DIM
    return jnp.where(lo, v, rolled), jnp.where(lo, rolled, v)


def _in_proj_kernel(x_ref, g_ref, w_ref, zqa_ref, zqb_ref, zkva_ref, zkvb_ref):
    s = pl.program_id(1)

    @pl.when(s == 0)
    def _():
        zkva_ref[0] = jnp.zeros(zkva_ref.shape[1:], zkva_ref.dtype)
        zkvb_ref[0] = jnp.zeros(zkvb_ref.shape[1:], zkvb_ref.dtype)

    def project_rows(rows):
        u = (_rms_scale(x_ref[0, rows, :]) * g_ref[...]).astype(jnp.bfloat16)

        def proj(off, width):
            return jnp.dot(u, w_ref[:, off:off + width], preferred_element_type=jnp.float32)

        bf = jnp.bfloat16
        zqa_ref[0, rows, 0:512] = (proj(OFF_QA, A_Q) * Q_SCALE).astype(bf)
        zqa_ref[0, rows, 512:1024] = _silu(proj(OFF_GA, A_W)).astype(bf)
        zqb_ref[0, rows, 0:512] = (proj(OFF_QB, B_W) * Q_SCALE).astype(bf)
        zqb_ref[0, rows, 512:1024] = _silu(proj(OFF_GB, B_W)).astype(bf)
        assert OFF_VA == OFF_KA + A_KV
        kv = proj(OFF_KA, 2 * A_KV)
        kd0, kd1 = _dup_halves(kv[:, :A_KV])
        vd0, vd1 = _dup_halves(kv[:, A_KV:])
        zkva_ref[0, rows, 0:128] = kd0.astype(bf)
        zkva_ref[0, rows, 128:256] = kd1.astype(bf)
        zkva_ref[0, rows, 256:384] = vd0.astype(bf)
        zkva_ref[0, rows, 384:512] = vd1.astype(bf)
        zkvb_ref[0, rows, 0:512] = proj(OFF_KB, B_W).astype(bf)
        zkvb_ref[0, rows, 512:1024] = proj(OFF_VB, B_W).astype(bf)

    @pl.when(s > 0)
    def _():
        for r0 in range(0, TOK_BLOCK, IN_SUB_BLOCK):
            project_rows(slice(r0, r0 + IN_SUB_BLOCK))


def _in_proj(x, norm_g, w_in_bf16):
    b, s, d = x.shape
    nblk = s // TOK_BLOCK
    pad_blk = KV_PAD // TOK_BLOCK
    assert pad_blk * TOK_BLOCK == KV_PAD and nblk * TOK_BLOCK == s
    tok_idx = lambda bi, si: (bi, jnp.maximum(si - pad_blk, 0), 0)
    pad_idx = lambda bi, si: (bi, si, 0)
    bf = jnp.bfloat16
    return pl.pallas_call(
        _in_proj_kernel,
        grid=(b, nblk + pad_blk),
        in_specs=[
            pl.BlockSpec((1, TOK_BLOCK, d), tok_idx),
            pl.BlockSpec((1, d), lambda bi, si: (0, 0)),
            pl.BlockSpec((d, D_IN_PROJ), lambda bi, si: (0, 0)),
        ],
        out_specs=[
            pl.BlockSpec((1, TOK_BLOCK, ZQ_W), tok_idx),
            pl.BlockSpec((1, TOK_BLOCK, ZQ_W), tok_idx),
            pl.BlockSpec((1, TOK_BLOCK, ZKVA_W), pad_idx),
            pl.BlockSpec((1, TOK_BLOCK, ZKVB_W), pad_idx),
        ],
        out_shape=[
            jax.ShapeDtypeStruct((b, s, ZQ_W), bf),
            jax.ShapeDtypeStruct((b, s, ZQ_W), bf),
            jax.ShapeDtypeStruct((b, s + KV_PAD, ZKVA_W), bf),
            jax.ShapeDtypeStruct((b, s + KV_PAD, ZKVB_W), bf),
        ],
        compiler_params=pltpu.CompilerParams(
            dimension_semantics=("arbitrary", "arbitrary"),
            vmem_limit_bytes=VMEM_LIMIT),
        name="in_proj",
    )(x, norm_g, w_in_bf16)


def _band_geometry(width, n_prev):
    i = lax.broadcasted_iota(jnp.int32, (PAIR, width), 0)
    jk = lax.broadcasted_iota(jnp.int32, (PAIR, width), 1)
    ci, kc = i // CHUNK, jk // CHUNK
    return i, jk, (kc >= ci) & (kc <= ci + n_prev)


def _pv_with_row_sums(e, v):
    v_ones = jnp.concatenate([v, jnp.ones(v.shape, v.dtype)], axis=1)
    o = jnp.dot(e, v_ones, preferred_element_type=jnp.float32)
    return o[:, :LANES], o[:, LANES:]


def _software_pipeline(n, units):
    u, lag = PIPE_UNROLL, PIPE_LAG
    assert n % u == 0 and n >= 2 * u and 2 * lag <= u

    def step(base, j, stage_range):
        for first, second, third in units:
            if 0 in stage_range:
                first(base + j, j % u)
            if 1 in stage_range:
                second(base + j - lag, (j - lag) % u)
            if 2 in stage_range:
                third(base + j - 2 * lag, (j - 2 * lag) % u)

    for j in range(u):
        step(0, j, [k for k in range(3) if j >= k * lag])

    def steady(t, carry):
        for j in range(u):
            step(t * u, j, range(3))
        return carry

    lax.fori_loop(1, n // u, steady, 0)
    for j in range(u, u + 2 * lag):
        step(n - u, j, [k for k in range(1, 3) if j - k * lag < u])


def _head_masks():
    lane = lax.broadcasted_iota(jnp.int32, (PAIR, LANES), 1)
    return lane < HEAD_DIM


def _split_heads(qp, lo):
    zero = jnp.zeros((), qp.dtype)
    return jnp.concatenate([jnp.where(lo, qp, zero), jnp.where(lo, zero, qp)], axis=0)


def _merge_heads(o, lo):
    return jnp.where(lo, o[0:PAIR], o[PAIR:2 * PAIR])


def _pair_start(pp):
    return pp * PAIR if isinstance(pp, int) else pl.multiple_of(pp * PAIR, PAIR)


def _q_rows(pp):
    return pl.ds(_pair_start(pp), PAIR)


def _band_rows(pp, n_prev, width):
    return pl.ds(_pair_start(pp) + (KV_PAD - n_prev * CHUNK), width)


def _mask_before_sequence(s, pp, n_prev):
    if not isinstance(pp, int) or pp * PAIR >= n_prev * CHUNK:
        return s
    jk = lax.broadcasted_iota(jnp.int32, s.shape, 1)
    return jnp.where(jk < n_prev * CHUNK - pp * PAIR, NEG_BIG, s)


def _attn_a_kernel(zq_ref, zkv_ref, sink_ref, y_ref, bias_ref, s_ref, p_ref, t_ref):
    n_pairs = zq_ref.shape[1] // PAIR

    @pl.when(pl.program_id(0) == 0)
    def _():
        i, jk, valid = _band_geometry(A_BAND, A_PREV)
        dist = jnp.abs(i - jk + A_PREV * CHUNK).astype(jnp.float32)
        for h in range(A_HEADS):
            slope = 2.0 ** (-8.0 * (h + 1) / A_HEADS) * LOG2E
            rows = slice((h % A_GROUP) * PAIR, (h % A_GROUP + 1) * PAIR)
            bias_ref[h // A_GROUP, rows, :] = jnp.where(valid, -slope * dist, NEG_BIG)

    lo = _head_masks()

    def unit(g):
        def scores(pp, slot):
            rows = _q_rows(pp)
            q0 = zq_ref[0, rows, 256 * g:256 * g + 128]
            q1 = zq_ref[0, rows, 256 * g + 128:256 * g + 256]
            lhs = jnp.concatenate([_split_heads(q0, lo), _split_heads(q1, lo)], axis=0)
            k = zkv_ref[0, _band_rows(pp, A_PREV, A_BAND), 128 * g:128 * g + 128]
            s = lax.dot_general(lhs, k, NT_DIMS, preferred_element_type=jnp.float32)
            s_ref[g, slot] = _mask_before_sequence(s + bias_ref[g], pp, A_PREV)

        def weights(pp, slot):
            m = jnp.max(s_ref[g, slot], axis=-1, keepdims=True)
            m = jnp.broadcast_to(m, (A_GROUP * PAIR, LANES))
            for t in range(A_BAND // LANES):
                cols = slice(t * LANES, (t + 1) * LANES)
                p_ref[g, slot, :, cols] = jnp.exp2(s_ref[g, slot, :, cols] - m).astype(jnp.bfloat16)
            t_ref[g, slot] = jnp.exp2(sink_ref[g] - m)

        def output(pp, slot):
            rows = _q_rows(pp)
            v = zkv_ref[0, _band_rows(pp, A_PREV, A_BAND), 256 + 128 * g:256 + 128 * g + 128]
            o, denom = _pv_with_row_sums(p_ref[g, slot], v)
            denom = denom + t_ref[g, slot]
            o = jnp.concatenate([_merge_heads(o[0:2 * PAIR], lo),
                                 _merge_heads(o[2 * PAIR:4 * PAIR], lo)], axis=1)
            denom = jnp.concatenate([_merge_heads(denom[0:2 * PAIR], lo),
                                     _merge_heads(denom[2 * PAIR:4 * PAIR], lo)], axis=1)
            gate = zq_ref[0, rows, 512 + 256 * g:512 + 256 * g + 256].astype(jnp.float32)
            y_ref[0, rows, 256 * g:256 * g + 256] = (o / denom * gate).astype(y_ref.dtype)

        return scores, weights, output

    _software_pipeline(n_pairs, [unit(g) for g in range(A_KV_HEADS)])


def _attention_a(zq, zkv, sink):
    b, s, _ = zq.shape
    slots = PIPE_UNROLL
    rows = A_GROUP * PAIR
    return pl.pallas_call(
        _attn_a_kernel,
        grid=(b,),
        in_specs=[
            pl.BlockSpec((1, s, ZQ_W), lambda bi: (bi, 0, 0)),
            pl.BlockSpec((1, s + KV_PAD, ZKVA_W), lambda bi: (bi, 0, 0)),
            pl.BlockSpec(sink.shape, lambda bi: (0, 0, 0)),
        ],
        out_specs=pl.BlockSpec((1, s, A_W), lambda bi: (bi, 0, 0)),
        out_shape=jax.ShapeDtypeStruct((b, s, A_W), jnp.bfloat16),
        scratch_shapes=[
            pltpu.VMEM((A_KV_HEADS, rows, A_BAND), jnp.float32),
            pltpu.VMEM((A_KV_HEADS, slots, rows, A_BAND), jnp.float32),
            pltpu.VMEM((A_KV_HEADS, slots, rows, A_BAND), jnp.bfloat16),
            pltpu.VMEM((A_KV_HEADS, slots, rows, LANES), jnp.float32),
        ],
        compiler_params=pltpu.CompilerParams(
            dimension_semantics=("arbitrary",),
            vmem_limit_bytes=VMEM_LIMIT),
        name="attention_a",
    )(zq, zkv, sink)


def _attn_b_kernel(zq_ref, zkv_ref, b0_ref, y_ref, bias_ref, s_ref, p_ref):
    n_pairs = zq_ref.shape[1] // PAIR

    @pl.when(pl.program_id(0) == 0)
    def _():
        _, _, valid = _band_geometry(B_BAND, B_PREV)
        for h in range(B_HEADS):
            row = jnp.broadcast_to(b0_ref[h:h + 1, :], (PAIR, TOEP_W))
            toep = pltpu.roll(row, 0, axis=1, stride=1, stride_axis=0)[:, :B_BAND]
            rows = slice((h % 2) * PAIR, (h % 2 + 1) * PAIR)
            bias_ref[h // 2, rows, :] = jnp.where(valid, toep * LOG2E, NEG_BIG)

    lo = _head_masks()

    def unit(hp):
        def scores(pp, slot):
            qp = zq_ref[0, _q_rows(pp), 128 * hp:128 * hp + 128]
            k = zkv_ref[0, _band_rows(pp, B_PREV, B_BAND), 128 * hp:128 * hp + 128]
            s = lax.dot_general(_split_heads(qp, lo), k, NT_DIMS,
                                preferred_element_type=jnp.float32)
            s_ref[hp, slot] = _mask_before_sequence(s + bias_ref[hp], pp, B_PREV)

        def weights(pp, slot):
            s = s_ref[hp, slot]
            m = jnp.max(s, axis=-1, keepdims=True)
            p_ref[hp, slot] = jnp.exp2(s - m).astype(jnp.bfloat16)

        def output(pp, slot):
            rows = _q_rows(pp)
            v = zkv_ref[0, _band_rows(pp, B_PREV, B_BAND), 512 + 128 * hp:512 + 128 * hp + 128]
            o, denom = _pv_with_row_sums(p_ref[hp, slot], v)
            gate = zq_ref[0, rows, 512 + 128 * hp:512 + 128 * hp + 128].astype(jnp.float32)
            y_ref[0, rows, 128 * hp:128 * hp + 128] = (
                _merge_heads(o, lo) / _merge_heads(denom, lo) * gate).astype(y_ref.dtype)

        return scores, weights, output

    _software_pipeline(n_pairs, [unit(hp) for hp in range(B_HEADS // 2)])


def _toeplitz_base_rows(rel_bias_b):
    tbl = rel_bias_b.astype(jnp.float32)
    h = tbl.shape[0]
    far = tbl[:, 2 * MAX_REL:2 * MAX_REL + 1]
    n_far = B_PREV * CHUNK - MAX_REL
    n_mid = B_BAND - n_far
    assert n_mid <= 2 * MAX_REL
    mid = tbl[:, 2 * MAX_REL:2 * MAX_REL - n_mid:-1]
    return jnp.concatenate(
        [jnp.broadcast_to(far, (h, n_far)), mid, jnp.broadcast_to(far, (h, TOEP_W - B_BAND))], axis=1)


def _attention_b(zq, zkv, b0):
    b, s, _ = zq.shape
    slots = PIPE_UNROLL
    units = B_HEADS // 2
    return pl.pallas_call(
        _attn_b_kernel,
        grid=(b,),
        in_specs=[
            pl.BlockSpec((1, s, ZQ_W), lambda bi: (bi, 0, 0)),
            pl.BlockSpec((1, s + KV_PAD, ZKVB_W), lambda bi: (bi, 0, 0)),
            pl.BlockSpec(b0.shape, lambda bi: (0, 0)),
        ],
        out_specs=pl.BlockSpec((1, s, B_W), lambda bi: (bi, 0, 0)),
        out_shape=jax.ShapeDtypeStruct((b, s, B_W), jnp.bfloat16),
        scratch_shapes=[
            pltpu.VMEM((units, 2 * PAIR, B_BAND), jnp.float32),
            pltpu.VMEM((units, slots, 2 * PAIR, B_BAND), jnp.float32),
            pltpu.VMEM((units, slots, 2 * PAIR, B_BAND), jnp.bfloat16),
        ],
        compiler_params=pltpu.CompilerParams(
            dimension_semantics=("arbitrary",),
            vmem_limit_bytes=VMEM_LIMIT),
        name="attention_b",
    )(zq, zkv, b0)


def _out_proj_kernel(x_ref, ya_ref, yb_ref, p_ref, w_out_ref, ple_g_ref, w_gate_ref, w_proj_ref,
                     fin_g_ref, o_ref):
    for r0 in range(0, TOK_BLOCK, OUT_SUB_BLOCK):
        rows = slice(r0, r0 + OUT_SUB_BLOCK)
        y = jnp.concatenate([ya_ref[rows, :], yb_ref[rows, :]], axis=1)
        h = x_ref[rows, :] + jnp.dot(y, w_out_ref[...], preferred_element_type=jnp.float32)
        u = (_rms_scale(h) * ple_g_ref[...]).astype(jnp.bfloat16)
        gl = jnp.dot(u, w_gate_ref[...], preferred_element_type=jnp.float32)
        gate = 1.0 / (1.0 + jnp.exp(-gl))
        pe = jnp.dot(p_ref[rows, :].astype(jnp.bfloat16), w_proj_ref[...],
                     preferred_element_type=jnp.float32)
        h = h + pe * gate
        o_ref[rows, :] = _rms_scale(h) * fin_g_ref[...]


def _out_proj(x2, ya2, yb2, p2, w_out, ple_g, w_gate, w_proj, fin_g):
    n, d = x2.shape
    pd = p2.shape[1]
    tok = lambda i: (i, 0)
    const = lambda i: (0, 0)
    return pl.pallas_call(
        _out_proj_kernel,
        grid=(n // TOK_BLOCK,),
        in_specs=[
            pl.BlockSpec((TOK_BLOCK, d), tok),
            pl.BlockSpec((TOK_BLOCK, ya2.shape[1]), tok),
            pl.BlockSpec((TOK_BLOCK, yb2.shape[1]), tok),
            pl.BlockSpec((TOK_BLOCK, pd), tok),
            pl.BlockSpec((d, d), const),
            pl.BlockSpec((1, d), const),
            pl.BlockSpec((d, d), const),
            pl.BlockSpec((pd, d), const),
            pl.BlockSpec((1, d), const),
        ],
        out_specs=pl.BlockSpec((TOK_BLOCK, d), tok),
        out_shape=jax.ShapeDtypeStruct((n, d), jnp.float32),
        compiler_params=pltpu.CompilerParams(
            dimension_semantics=("arbitrary",),
            vmem_limit_bytes=VMEM_LIMIT),
        name="out_proj",
    )(x2, ya2, yb2, p2, w_out, ple_g, w_gate, w_proj, fin_g)


@jax.jit
def kernel(x, p, norm_g, w_in, sink_a, rel_bias_b, w_out, ple_norm_g, w_ple_proj, w_ple_gate,
           final_norm_g):
    b, s, d = x.shape
    bf = jnp.bfloat16
    assert norm_g.shape[0] == 1, "the final RMSNorm is fused into the (single) layer's output kernel"
    zqa, zqb, zkva, zkvb = _in_proj(x, norm_g[0][None], w_in[0].astype(bf))
    sink = jnp.repeat(sink_a[0].astype(jnp.float32) * LOG2E, PAIR)
    sink = jnp.broadcast_to(sink.reshape(A_KV_HEADS, A_GROUP * PAIR, 1),
                            (A_KV_HEADS, A_GROUP * PAIR, LANES))
    ya = _attention_a(zqa, zkva, sink)
    yb = _attention_b(zqb, zkvb, _toeplitz_base_rows(rel_bias_b[0]))
    out = _out_proj(x.reshape(b * s, d), ya.reshape(b * s, -1), yb.reshape(b * s, -1),
                    p[0].reshape(b * s, -1), w_out[0].astype(bf), ple_norm_g[0][None],
                    w_ple_gate[0].astype(bf), w_ple_proj[0].astype(bf), final_norm_g[None])
    return out.reshape(b, s, d)
```

```python
import math

import jax
import jax.numpy as jnp
from jax import lax
from jax.experimental import pallas as pl
from jax.experimental.pallas import tpu as pltpu

D_MODEL = 1024
CHUNK = 64
HEAD_DIM = 64
A_HEADS = 8
A_KV_HEADS = 2
A_GROUP = A_HEADS // A_KV_HEADS
B_HEADS = 8
A_PREV = 2
B_PREV = 8
MAX_REL = 128
RMS_EPS = 1e-6
NEG_BIG = -1e30
LOG2E = math.log2(math.e)

A_Q = A_HEADS * HEAD_DIM
A_KV = A_KV_HEADS * HEAD_DIM
A_W = 512
B_W = 512
OFF_QA, OFF_KA, OFF_VA, OFF_GA = 0, 512, 640, 768
OFF_QB, OFF_KB, OFF_VB, OFF_GB = 1280, 1792, 2304, 2816
D_IN_PROJ = 3328

LANES = 128
PAIR = 2 * CHUNK
A_BAND = (A_PREV + 2) * CHUNK
B_BAND = (B_PREV + 2) * CHUNK
TOK_BLOCK = 1024
IN_SUB_BLOCK = 256
OUT_SUB_BLOCK = 512
TOEP_W = B_BAND + PAIR
ZQ_W = 1024
ZKVA_W = 512
ZKVB_W = 1024
PIPE_UNROLL = 4
PIPE_LAG = 2
VMEM_LIMIT = 56 * 1024 * 1024
Q_SCALE = HEAD_DIM ** -0.5 * LOG2E
NT_DIMS = (((1,), (1,)), ((), ()))


def _rms_scale(xf):
    var = jnp.mean(xf * xf, axis=-1, keepdims=True)
    return xf * lax.rsqrt(var + RMS_EPS)


def _silu(g):
    return g * (1.0 / (1.0 + jnp.exp(-g)))


def _dup_halves(v):
    rolled = pltpu.roll(v, HEAD_DIM, axis=1)
    lane = lax.broadcasted_iota(jnp.int32, v.shape, 1)
    lo = lane < HEAD_DIM
    return jnp.where(lo, v, rolled), jnp.where(lo, rolled, v)


def _in_proj_kernel(x_ref, g_ref, w_ref, zqa_ref, zqb_ref, zkva_ref, zkvb_ref):
    def project_rows(rows):
        u = (_rms_scale(x_ref[0, rows, :]) * g_ref[...]).astype(jnp.bfloat16)

        def proj(off, width):
            return jnp.dot(u, w_ref[:, off:off + width], preferred_element_type=jnp.float32)

        bf = jnp.bfloat16
        zqa_ref[0, rows, 0:512] = (proj(OFF_QA, A_Q) * Q_SCALE).astype(bf)
        zqa_ref[0, rows, 512:1024] = _silu(proj(OFF_GA, A_W)).astype(bf)
        zqb_ref[0, rows, 0:512] = (proj(OFF_QB, B_W) * Q_SCALE).astype(bf)
        zqb_ref[0, rows, 512:1024] = _silu(proj(OFF_GB, B_W)).astype(bf)
        assert OFF_VA == OFF_KA + A_KV
        kv = proj(OFF_KA, 2 * A_KV)
        kd0, kd1 = _dup_halves(kv[:, :A_KV])
        vd0, vd1 = _dup_halves(kv[:, A_KV:])
        zkva_ref[0, rows, 0:128] = kd0.astype(bf)
        zkva_ref[0, rows, 128:256] = kd1.astype(bf)
        zkva_ref[0, rows, 256:384] = vd0.astype(bf)
        zkva_ref[0, rows, 384:512] = vd1.astype(bf)
        zkvb_ref[0, rows, 0:512] = proj(OFF_KB, B_W).astype(bf)
        zkvb_ref[0, rows, 512:1024] = proj(OFF_VB, B_W).astype(bf)

    for r0 in range(0, TOK_BLOCK, IN_SUB_BLOCK):
        project_rows(slice(r0, r0 + IN_SUB_BLOCK))


def _in_proj(x, norm_g, w_in_bf16):
    b, s, d = x.shape
    nblk = s // TOK_BLOCK
    assert nblk * TOK_BLOCK == s
    tok_idx = lambda bi, si: (bi, si, 0)
    bf = jnp.bfloat16
    return pl.pallas_call(
        _in_proj_kernel,
        grid=(b, nblk),
        in_specs=[
            pl.BlockSpec((1, TOK_BLOCK, d), tok_idx),
            pl.BlockSpec((1, d), lambda bi, si: (0, 0)),
            pl.BlockSpec((d, D_IN_PROJ), lambda bi, si: (0, 0)),
        ],
        out_specs=[
            pl.BlockSpec((1, TOK_BLOCK, ZQ_W), tok_idx),
            pl.BlockSpec((1, TOK_BLOCK, ZQ_W), tok_idx),
            pl.BlockSpec((1, TOK_BLOCK, ZKVA_W), tok_idx),
            pl.BlockSpec((1, TOK_BLOCK, ZKVB_W), tok_idx),
        ],
        out_shape=[
            jax.ShapeDtypeStruct((b, s, ZQ_W), bf),
            jax.ShapeDtypeStruct((b, s, ZQ_W), bf),
            jax.ShapeDtypeStruct((b, s, ZKVA_W), bf),
            jax.ShapeDtypeStruct((b, s, ZKVB_W), bf),
        ],
        compiler_params=pltpu.CompilerParams(
            dimension_semantics=("arbitrary", "arbitrary"),
            vmem_limit_bytes=VMEM_LIMIT),
        name="in_proj",
    )(x, norm_g, w_in_bf16)


def _band_geometry(width, n_prev):
    i = lax.broadcasted_iota(jnp.int32, (PAIR, width), 0)
    jk = lax.broadcasted_iota(jnp.int32, (PAIR, width), 1)
    ci, kc = i // CHUNK, jk // CHUNK
    return i, jk, (kc >= ci) & (kc <= ci + n_prev)


def _pv_with_row_sums(e, v):
    v_ones = jnp.concatenate([v, jnp.ones(v.shape, v.dtype)], axis=1)
    o = jnp.dot(e, v_ones, preferred_element_type=jnp.float32)
    return o[:, :LANES], o[:, LANES:]


def _software_pipeline(n, units):
    u, lag = PIPE_UNROLL, PIPE_LAG
    fill = 2 * u
    assert n % u == 0 and n >= fill + u and 2 * lag <= u

    def step(base, j, stage_range):
        for first, second, third in units:
            if 0 in stage_range:
                first(base + j, j % u)
            if 1 in stage_range:
                second(base + j - lag, (j - lag) % u)
            if 2 in stage_range:
                third(base + j - 2 * lag, (j - 2 * lag) % u)

    for j in range(fill):
        step(0, j, [k for k in range(3) if j >= k * lag])

    def steady(t, carry):
        for j in range(u):
            step(t * u, j, range(3))
        return carry

    lax.fori_loop(fill // u, n // u, steady, 0)
    for j in range(u, u + 2 * lag):
        step(n - u, j, [k for k in range(1, 3) if j - k * lag < u])


def _head_masks():
    lane = lax.broadcasted_iota(jnp.int32, (PAIR, LANES), 1)
    return lane < HEAD_DIM


def _split_heads(qp, lo):
    zero = jnp.zeros((), qp.dtype)
    return jnp.concatenate([jnp.where(lo, qp, zero), jnp.where(lo, zero, qp)], axis=0)


def _merge_heads(o, lo):
    return jnp.where(lo, o[0:PAIR], o[PAIR:2 * PAIR])


def _pair_start(pp):
    return pp * PAIR if isinstance(pp, int) else pl.multiple_of(pp * PAIR, PAIR)


def _q_rows(pp):
    return pl.ds(_pair_start(pp), PAIR)


def _band(pp, n_prev, width):
    if isinstance(pp, int):
        start = pp * PAIR - n_prev * CHUNK
        cut = max(0, -start)
        return pl.ds(start + cut, width - cut), cut
    assert (n_prev * CHUNK) % PAIR == 0
    return pl.ds(pl.multiple_of(pp * PAIR - n_prev * CHUNK, PAIR), width), 0


def _attn_a_kernel(zq_ref, zkv_ref, sink_ref, y_ref, bias_ref, s_ref, p_ref, t_ref):
    n_pairs = zq_ref.shape[1] // PAIR

    @pl.when(pl.program_id(0) == 0)
    def _():
        i, jk, valid = _band_geometry(A_BAND, A_PREV)
        dist = jnp.abs(i - jk + A_PREV * CHUNK).astype(jnp.float32)
        for h in range(A_HEADS):
            slope = 2.0 ** (-8.0 * (h + 1) / A_HEADS) * LOG2E
            rows = slice((h % A_GROUP) * PAIR, (h % A_GROUP + 1) * PAIR)
            bias_ref[h // A_GROUP, rows, :] = jnp.where(valid, -slope * dist, NEG_BIG)

    lo = _head_masks()

    def unit(g):
        def scores(pp, slot):
            rows = _q_rows(pp)
            q0 = zq_ref[0, rows, 256 * g:256 * g + 128]
            q1 = zq_ref[0, rows, 256 * g + 128:256 * g + 256]
            lhs = jnp.concatenate([_split_heads(q0, lo), _split_heads(q1, lo)], axis=0)
            band, cut = _band(pp, A_PREV, A_BAND)
            k = zkv_ref[0, band, 128 * g:128 * g + 128]
            s = lax.dot_general(lhs, k, NT_DIMS, preferred_element_type=jnp.float32)
            s_ref[g, slot, :, cut:] = s + bias_ref[g, :, cut:]

        def weights(pp, slot):
            _, cut = _band(pp, A_PREV, A_BAND)
            m = jnp.max(s_ref[g, slot, :, cut:], axis=-1, keepdims=True)
            m = jnp.broadcast_to(m, (A_GROUP * PAIR, LANES))
            for t in range(cut // LANES, A_BAND // LANES):
                cols = slice(t * LANES, (t + 1) * LANES)
                p_ref[g, slot, :, cols] = jnp.exp2(s_ref[g, slot, :, cols] - m).astype(jnp.bfloat16)
            t_ref[g, slot] = jnp.exp2(sink_ref[g] - m)

        def output(pp, slot):
            rows = _q_rows(pp)
            band, cut = _band(pp, A_PREV, A_BAND)
            v = zkv_ref[0, band, 256 + 128 * g:256 + 128 * g + 128]
            o, denom = _pv_with_row_sums(p_ref[g, slot, :, cut:], v)
            denom = denom + t_ref[g, slot]
            o = jnp.concatenate([_merge_heads(o[0:2 * PAIR], lo),
                                 _merge_heads(o[2 * PAIR:4 * PAIR], lo)], axis=1)
            denom = jnp.concatenate([_merge_heads(denom[0:2 * PAIR], lo),
                                     _merge_heads(denom[2 * PAIR:4 * PAIR], lo)], axis=1)
            gate = zq_ref[0, rows, 512 + 256 * g:512 + 256 * g + 256].astype(jnp.float32)
            y_ref[0, rows, 256 * g:256 * g + 256] = (o / denom * gate).astype(y_ref.dtype)

        return scores, weights, output

    _software_pipeline(n_pairs, [unit(g) for g in range(A_KV_HEADS)])


def _attention_a(zq, zkv, sink):
    b, s, _ = zq.shape
    slots = PIPE_UNROLL
    rows = A_GROUP * PAIR
    return pl.pallas_call(
        _attn_a_kernel,
        grid=(b,),
        in_specs=[
            pl.BlockSpec((1, s, ZQ_W), lambda bi: (bi, 0, 0)),
            pl.BlockSpec((1, s, ZKVA_W), lambda bi: (bi, 0, 0)),
            pl.BlockSpec(sink.shape, lambda bi: (0, 0, 0)),
        ],
        out_specs=pl.BlockSpec((1, s, A_W), lambda bi: (bi, 0, 0)),
        out_shape=jax.ShapeDtypeStruct((b, s, A_W), jnp.bfloat16),
        scratch_shapes=[
            pltpu.VMEM((A_KV_HEADS, rows, A_BAND), jnp.float32),
            pltpu.VMEM((A_KV_HEADS, slots, rows, A_BAND), jnp.float32),
            pltpu.VMEM((A_KV_HEADS, slots, rows, A_BAND), jnp.bfloat16),
            pltpu.VMEM((A_KV_HEADS, slots, rows, LANES), jnp.float32),
        ],
        compiler_params=pltpu.CompilerParams(
            dimension_semantics=("arbitrary",),
            vmem_limit_bytes=VMEM_LIMIT),
        name="attention_a",
    )(zq, zkv, sink)


def _attn_b_kernel(zq_ref, zkv_ref, b0_ref, y_ref, bias_ref, s_ref, p_ref):
    n_pairs = zq_ref.shape[1] // PAIR

    @pl.when(pl.program_id(0) == 0)
    def _():
        _, _, valid = _band_geometry(B_BAND, B_PREV)
        for h in range(B_HEADS):
            row = jnp.broadcast_to(b0_ref[h:h + 1, :], (PAIR, TOEP_W))
            toep = pltpu.roll(row, 0, axis=1, stride=1, stride_axis=0)[:, :B_BAND]
            rows = slice((h % 2) * PAIR, (h % 2 + 1) * PAIR)
            bias_ref[h // 2, rows, :] = jnp.where(valid, toep * LOG2E, NEG_BIG)

    lo = _head_masks()

    def unit(hp):
        def scores(pp, slot):
            qp = zq_ref[0, _q_rows(pp), 128 * hp:128 * hp + 128]
            band, cut = _band(pp, B_PREV, B_BAND)
            k = zkv_ref[0, band, 128 * hp:128 * hp + 128]
            s = lax.dot_general(_split_heads(qp, lo), k, NT_DIMS,
                                preferred_element_type=jnp.float32)
            s_ref[hp, slot, :, cut:] = s + bias_ref[hp, :, cut:]

        def weights(pp, slot):
            _, cut = _band(pp, B_PREV, B_BAND)
            s = s_ref[hp, slot, :, cut:]
            m = jnp.max(s, axis=-1, keepdims=True)
            p_ref[hp, slot, :, cut:] = jnp.exp2(s - m).astype(jnp.bfloat16)

        def output(pp, slot):
            rows = _q_rows(pp)
            band, cut = _band(pp, B_PREV, B_BAND)
            v = zkv_ref[0, band, 512 + 128 * hp:512 + 128 * hp + 128]
            o, denom = _pv_with_row_sums(p_ref[hp, slot, :, cut:], v)
            gate = zq_ref[0, rows, 512 + 128 * hp:512 + 128 * hp + 128].astype(jnp.float32)
            y_ref[0, rows, 128 * hp:128 * hp + 128] = (
                _merge_heads(o, lo) / _merge_heads(denom, lo) * gate).astype(y_ref.dtype)

        return scores, weights, output

    _software_pipeline(n_pairs, [unit(hp) for hp in range(B_HEADS // 2)])


def _toeplitz_base_rows(rel_bias_b):
    tbl = rel_bias_b.astype(jnp.float32)
    h = tbl.shape[0]
    far = tbl[:, 2 * MAX_REL:2 * MAX_REL + 1]
    n_far = B_PREV * CHUNK - MAX_REL
    n_mid = B_BAND - n_far
    assert n_mid <= 2 * MAX_REL
    mid = tbl[:, 2 * MAX_REL:2 * MAX_REL - n_mid:-1]
    return jnp.concatenate(
        [jnp.broadcast_to(far, (h, n_far)), mid, jnp.broadcast_to(far, (h, TOEP_W - B_BAND))], axis=1)


def _attention_b(zq, zkv, b0):
    b, s, _ = zq.shape
    slots = PIPE_UNROLL
    units = B_HEADS // 2
    return pl.pallas_call(
        _attn_b_kernel,
        grid=(b,),
        in_specs=[
            pl.BlockSpec((1, s, ZQ_W), lambda bi: (bi, 0, 0)),
            pl.BlockSpec((1, s, ZKVB_W), lambda bi: (bi, 0, 0)),
            pl.BlockSpec(b0.shape, lambda bi: (0, 0)),
        ],
        out_specs=pl.BlockSpec((1, s, B_W), lambda bi: (bi, 0, 0)),
        out_shape=jax.ShapeDtypeStruct((b, s, B_W), jnp.bfloat16),
        scratch_shapes=[
            pltpu.VMEM((units, 2 * PAIR, B_BAND), jnp.float32),
            pltpu.VMEM((units, slots, 2 * PAIR, B_BAND), jnp.float32),
            pltpu.VMEM((units, slots, 2 * PAIR, B_BAND), jnp.bfloat16),
        ],
        compiler_params=pltpu.CompilerParams(
            dimension_semantics=("arbitrary",),
            vmem_limit_bytes=VMEM_LIMIT),
        name="attention_b",
    )(zq, zkv, b0)


def _out_proj_kernel(x_ref, ya_ref, yb_ref, p_ref, w_out_ref, ple_g_ref, w_gate_ref, w_proj_ref,
                     fin_g_ref, o_ref):
    for r0 in range(0, TOK_BLOCK, OUT_SUB_BLOCK):
        rows = slice(r0, r0 + OUT_SUB_BLOCK)
        y = jnp.concatenate([ya_ref[rows, :], yb_ref[rows, :]], axis=1)
        h = x_ref[rows, :] + jnp.dot(y, w_out_ref[...], preferred_element_type=jnp.float32)
        u = (_rms_scale(h) * ple_g_ref[...]).astype(jnp.bfloat16)
        gl = jnp.dot(u, w_gate_ref[...], preferred_element_type=jnp.float32)
        gate = 1.0 / (1.0 + jnp.exp(-gl))
        pe = jnp.dot(p_ref[rows, :].astype(jnp.bfloat16), w_proj_ref[...],
                     preferred_element_type=jnp.float32)
        h = h + pe * gate
        o_ref[rows, :] = _rms_scale(h) * fin_g_ref[...]


def _out_proj(x2, ya2, yb2, p2, w_out, ple_g, w_gate, w_proj, fin_g):
    n, d = x2.shape
    pd = p2.shape[1]
    tok = lambda i: (i, 0)
    const = lambda i: (0, 0)
    return pl.pallas_call(
        _out_proj_kernel,
        grid=(n // TOK_BLOCK,),
        in_specs=[
            pl.BlockSpec((TOK_BLOCK, d), tok),
            pl.BlockSpec((TOK_BLOCK, ya2.shape[1]), tok),
            pl.BlockSpec((TOK_BLOCK, yb2.shape[1]), tok),
            pl.BlockSpec((TOK_BLOCK, pd), tok),
            pl.BlockSpec((d, d), const),
            pl.BlockSpec((1, d), const),
            pl.BlockSpec((d, d), const),
            pl.BlockSpec((pd, d), const),
            pl.BlockSpec((1, d), const),
        ],
        out_specs=pl.BlockSpec((TOK_BLOCK, d), tok),
        out_shape=jax.ShapeDtypeStruct((n, d), jnp.float32),
        compiler_params=pltpu.CompilerParams(
            dimension_semantics=("arbitrary",),
            vmem_limit_bytes=VMEM_LIMIT),
        name="out_proj",
    )(x2, ya2, yb2, p2, w_out, ple_g, w_gate, w_proj, fin_g)


@jax.jit
def kernel(x, p, norm_g, w_in, sink_a, rel_bias_b, w_out, ple_norm_g, w_ple_proj, w_ple_gate,
           final_norm_g):
    b, s, d = x.shape
    bf = jnp.bfloat16
    assert norm_g.shape[0] == 1, "the final RMSNorm is fused into the (single) layer's output kernel"
    zqa, zqb, zkva, zkvb = _in_proj(x, norm_g[0][None], w_in[0].astype(bf))
    sink = jnp.repeat(sink_a[0].astype(jnp.float32) * LOG2E, PAIR)
    sink = jnp.broadcast_to(sink.reshape(A_KV_HEADS, A_GROUP * PAIR, 1),
                            (A_KV_HEADS, A_GROUP * PAIR, LANES))
    ya = _attention_a(zqa, zkva, sink)
    yb = _attention_b(zqb, zkvb, _toeplitz_base_rows(rel_bias_b[0]))
    out = _out_proj(x.reshape(b * s, d), ya.reshape(b * s, -1), yb.reshape(b * s, -1),
                    p[0].reshape(b * s, -1), w_out[0].astype(bf), ple_norm_g[0][None],
                    w_ple_gate[0].astype(bf), w_ple_proj[0].astype(bf), final_norm_g[None])
    return out.reshape(b, s, d)
```

```python
import math

import jax
import jax.numpy as jnp
from jax import lax
from jax.experimental import pallas as pl
from jax.experimental.pallas import tpu as pltpu

D_MODEL = 1024
CHUNK = 64
HEAD_DIM = 64
A_HEADS = 8
A_KV_HEADS = 2
A_GROUP = A_HEADS // A_KV_HEADS
B_HEADS = 8
A_PREV = 2
B_PREV = 8
MAX_REL = 128
RMS_EPS = 1e-6
NEG_BIG = -1e30
LOG2E = math.log2(math.e)

A_Q = A_HEADS * HEAD_DIM
A_KV = A_KV_HEADS * HEAD_DIM
A_W = 512
B_W = 512
OFF_QA, OFF_KA, OFF_VA, OFF_GA = 0, 512, 640, 768
OFF_QB, OFF_KB, OFF_VB, OFF_GB = 1280, 1792, 2304, 2816
D_IN_PROJ = 3328

LANES = 128
PAIR = 2 * CHUNK
A_BAND = (A_PREV + 2) * CHUNK
B_BAND = (B_PREV + 2) * CHUNK
TOK_BLOCK = 1024
IN_SUB_BLOCK = 256
OUT_SUB_BLOCK = 512
TOEP_W = B_BAND + PAIR
ZQ_W = 1024
ZKVA_W = 512
ZKVB_W = 1536
PIPE_UNROLL = 4
PIPE_LAG = 2
VMEM_LIMIT = 56 * 1024 * 1024
Q_SCALE = HEAD_DIM ** -0.5 * LOG2E
NT_DIMS = (((1,), (1,)), ((), ()))


def _rms_scale(xf):
    var = jnp.mean(xf * xf, axis=-1, keepdims=True)
    return xf * lax.rsqrt(var + RMS_EPS)


def _silu(g):
    return g * (1.0 / (1.0 + jnp.exp(-g)))


def _dup_halves(v):
    rolled = pltpu.roll(v, HEAD_DIM, axis=1)
    lane = lax.broadcasted_iota(jnp.int32, v.shape, 1)
    lo = lane < HEAD_DIM
    return jnp.where(lo, v, rolled), jnp.where(lo, rolled, v)


def _in_proj_kernel(x_ref, g_ref, w_ref, zqa_ref, zqb_ref, zkva_ref, zkvb_ref):
    def project_rows(rows):
        u = (_rms_scale(x_ref[0, rows, :]) * g_ref[...]).astype(jnp.bfloat16)

        def proj(off, width):
            return jnp.dot(u, w_ref[:, off:off + width], preferred_element_type=jnp.float32)

        bf = jnp.bfloat16
        zqa_ref[0, rows, 0:512] = (proj(OFF_QA, A_Q) * Q_SCALE).astype(bf)
        zqa_ref[0, rows, 512:1024] = _silu(proj(OFF_GA, A_W)).astype(bf)
        zqb_ref[0, rows, 0:512] = (proj(OFF_QB, B_W) * Q_SCALE).astype(bf)
        zqb_ref[0, rows, 512:1024] = _silu(proj(OFF_GB, B_W)).astype(bf)
        assert OFF_VA == OFF_KA + A_KV
        kv = proj(OFF_KA, 2 * A_KV)
        kd0, kd1 = _dup_halves(kv[:, :A_KV])
        vd0, vd1 = _dup_halves(kv[:, A_KV:])
        zkva_ref[0, rows, 0:128] = kd0.astype(bf)
        zkva_ref[0, rows, 128:256] = kd1.astype(bf)
        zkva_ref[0, rows, 256:384] = vd0.astype(bf)
        zkva_ref[0, rows, 384:512] = vd1.astype(bf)
        zkvb_ref[0, rows, 0:512] = proj(OFF_KB, B_W).astype(bf)
        vb = proj(OFF_VB, B_W)
        even_head = lax.broadcasted_iota(jnp.int32, vb.shape, 1) % LANES < HEAD_DIM
        zkvb_ref[0, rows, 512:1024] = jnp.where(even_head, vb, 0.0).astype(bf)
        zkvb_ref[0, rows, 1024:1536] = jnp.where(even_head, 0.0, vb).astype(bf)

    for r0 in range(0, TOK_BLOCK, IN_SUB_BLOCK):
        project_rows(slice(r0, r0 + IN_SUB_BLOCK))


def _in_proj(x, norm_g, w_in_bf16):
    b, s, d = x.shape
    nblk = s // TOK_BLOCK
    assert nblk * TOK_BLOCK == s
    tok_idx = lambda bi, si: (bi, si, 0)
    bf = jnp.bfloat16
    return pl.pallas_call(
        _in_proj_kernel,
        grid=(b, nblk),
        in_specs=[
            pl.BlockSpec((1, TOK_BLOCK, d), tok_idx),
            pl.BlockSpec((1, d), lambda bi, si: (0, 0)),
            pl.BlockSpec((d, D_IN_PROJ), lambda bi, si: (0, 0)),
        ],
        out_specs=[
            pl.BlockSpec((1, TOK_BLOCK, ZQ_W), tok_idx),
            pl.BlockSpec((1, TOK_BLOCK, ZQ_W), tok_idx),
            pl.BlockSpec((1, TOK_BLOCK, ZKVA_W), tok_idx),
            pl.BlockSpec((1, TOK_BLOCK, ZKVB_W), tok_idx),
        ],
        out_shape=[
            jax.ShapeDtypeStruct((b, s, ZQ_W), bf),
            jax.ShapeDtypeStruct((b, s, ZQ_W), bf),
            jax.ShapeDtypeStruct((b, s, ZKVA_W), bf),
            jax.ShapeDtypeStruct((b, s, ZKVB_W), bf),
        ],
        compiler_params=pltpu.CompilerParams(
            dimension_semantics=("arbitrary", "arbitrary"),
            vmem_limit_bytes=VMEM_LIMIT),
        name="in_proj",
    )(x, norm_g, w_in_bf16)


def _band_geometry(width, n_prev):
    i = lax.broadcasted_iota(jnp.int32, (PAIR, width), 0)
    jk = lax.broadcasted_iota(jnp.int32, (PAIR, width), 1)
    ci, kc = i // CHUNK, jk // CHUNK
    return i, jk, (kc >= ci) & (kc <= ci + n_prev)


def _pv_with_row_sums(e, v):
    v_ones = jnp.concatenate([v, jnp.ones(v.shape, v.dtype)], axis=1)
    o = jnp.dot(e, v_ones, preferred_element_type=jnp.float32)
    return o[:, :LANES], o[:, LANES:]


def _software_pipeline(n, units):
    u, lag = PIPE_UNROLL, PIPE_LAG
    fill = 2 * u
    assert n % u == 0 and n >= fill + u and 2 * lag <= u

    def step(base, j, stage_range):
        for first, second, third in units:
            if 2 in stage_range:
                third(base + j - 2 * lag, (j - 2 * lag) % u)
            if 0 in stage_range:
                first(base + j, j % u)
            if 1 in stage_range:
                second(base + j - lag, (j - lag) % u)

    for j in range(fill):
        step(0, j, [k for k in range(3) if j >= k * lag])

    def steady(t, carry):
        for j in range(u):
            step(t * u, j, range(3))
        return carry

    lax.fori_loop(fill // u, n // u, steady, 0)
    for j in range(u, u + 2 * lag):
        step(n - u, j, [k for k in range(1, 3) if j - k * lag < u])


def _head_masks():
    lane = lax.broadcasted_iota(jnp.int32, (PAIR, LANES), 1)
    return lane < HEAD_DIM


def _split_heads(qp, lo):
    zero = jnp.zeros((), qp.dtype)
    return jnp.concatenate([jnp.where(lo, qp, zero), jnp.where(lo, zero, qp)], axis=0)


def _merge_heads(o, lo):
    return jnp.where(lo, o[0:PAIR], o[PAIR:2 * PAIR])


def _pair_start(pp):
    return pp * PAIR if isinstance(pp, int) else pl.multiple_of(pp * PAIR, PAIR)


def _q_rows(pp):
    return pl.ds(_pair_start(pp), PAIR)


def _band(pp, n_prev, width):
    if isinstance(pp, int):
        start = pp * PAIR - n_prev * CHUNK
        cut = max(0, -start)
        return pl.ds(start + cut, width - cut), cut
    assert (n_prev * CHUNK) % PAIR == 0
    return pl.ds(pl.multiple_of(pp * PAIR - n_prev * CHUNK, PAIR), width), 0


def _attn_a_kernel(zq_ref, zkv_ref, sink_ref, y_ref, bias_ref, s_ref, p_ref, t_ref):
    n_pairs = zq_ref.shape[1] // PAIR

    @pl.when(pl.program_id(0) == 0)
    def _():
        i, jk, valid = _band_geometry(A_BAND, A_PREV)
        dist = jnp.abs(i - jk + A_PREV * CHUNK).astype(jnp.float32)
        for h in range(A_HEADS):
            slope = 2.0 ** (-8.0 * (h + 1) / A_HEADS) * LOG2E
            rows = slice((h % A_GROUP) * PAIR, (h % A_GROUP + 1) * PAIR)
            bias_ref[h // A_GROUP, rows, :] = jnp.where(valid, -slope * dist, NEG_BIG)

    lo = _head_masks()

    def unit(g):
        def scores(pp, slot):
            rows = _q_rows(pp)
            q0 = zq_ref[0, rows, 256 * g:256 * g + 128]
            q1 = zq_ref[0, rows, 256 * g + 128:256 * g + 256]
            lhs = jnp.concatenate([_split_heads(q0, lo), _split_heads(q1, lo)], axis=0)
            band, cut = _band(pp, A_PREV, A_BAND)
            k = zkv_ref[0, band, 128 * g:128 * g + 128]
            s = lax.dot_general(lhs, k, NT_DIMS, preferred_element_type=jnp.float32)
            s_ref[g, slot, :, cut:] = s + bias_ref[g, :, cut:]

        def weights(pp, slot):
            _, cut = _band(pp, A_PREV, A_BAND)
            m = jnp.max(s_ref[g, slot, :, cut:], axis=-1, keepdims=True)
            m = jnp.broadcast_to(m, (A_GROUP * PAIR, LANES))
            for t in range(cut // LANES, A_BAND // LANES):
                cols = slice(t * LANES, (t + 1) * LANES)
                p_ref[g, slot, :, cols] = jnp.exp2(s_ref[g, slot, :, cols] - m).astype(jnp.bfloat16)
            t_ref[g, slot] = jnp.exp2(sink_ref[g] - m)

        def output(pp, slot):
            rows = _q_rows(pp)
            band, cut = _band(pp, A_PREV, A_BAND)
            v = zkv_ref[0, band, 256 + 128 * g:256 + 128 * g + 128]
            o, denom = _pv_with_row_sums(p_ref[g, slot, :, cut:], v)
            denom = denom + t_ref[g, slot]
            o = jnp.concatenate([_merge_heads(o[0:2 * PAIR], lo),
                                 _merge_heads(o[2 * PAIR:4 * PAIR], lo)], axis=1)
            denom = jnp.concatenate([_merge_heads(denom[0:2 * PAIR], lo),
                                     _merge_heads(denom[2 * PAIR:4 * PAIR], lo)], axis=1)
            gate = zq_ref[0, rows, 512 + 256 * g:512 + 256 * g + 256].astype(jnp.float32)
            y_ref[0, rows, 256 * g:256 * g + 256] = (o / denom * gate).astype(y_ref.dtype)

        return scores, weights, output

    _software_pipeline(n_pairs, [unit(g) for g in range(A_KV_HEADS)])


def _attention_a(zq, zkv, sink):
    b, s, _ = zq.shape
    slots = PIPE_UNROLL
    rows = A_GROUP * PAIR
    return pl.pallas_call(
        _attn_a_kernel,
        grid=(b,),
        in_specs=[
            pl.BlockSpec((1, s, ZQ_W), lambda bi: (bi, 0, 0)),
            pl.BlockSpec((1, s, ZKVA_W), lambda bi: (bi, 0, 0)),
            pl.BlockSpec(sink.shape, lambda bi: (0, 0, 0)),
        ],
        out_specs=pl.BlockSpec((1, s, A_W), lambda bi: (bi, 0, 0)),
        out_shape=jax.ShapeDtypeStruct((b, s, A_W), jnp.bfloat16),
        scratch_shapes=[
            pltpu.VMEM((A_KV_HEADS, rows, A_BAND), jnp.float32),
            pltpu.VMEM((A_KV_HEADS, slots, rows, A_BAND), jnp.float32),
            pltpu.VMEM((A_KV_HEADS, slots, rows, A_BAND), jnp.bfloat16),
            pltpu.VMEM((A_KV_HEADS, slots, rows, LANES), jnp.float32),
        ],
        compiler_params=pltpu.CompilerParams(
            dimension_semantics=("arbitrary",),
            vmem_limit_bytes=VMEM_LIMIT),
        name="attention_a",
    )(zq, zkv, sink)


def _attn_b_kernel(zq_ref, zkv_ref, b0_ref, y_ref, bias_ref, s_ref, m_ref, p_ref):
    n_pairs = zq_ref.shape[1] // PAIR

    @pl.when(pl.program_id(0) == 0)
    def _():
        _, _, valid = _band_geometry(B_BAND, B_PREV)
        for h in range(B_HEADS):
            row = jnp.broadcast_to(b0_ref[h:h + 1, :], (PAIR, TOEP_W))
            toep = pltpu.roll(row, 0, axis=1, stride=1, stride_axis=0)[:, :B_BAND]
            rows = slice((h % 2) * PAIR, (h % 2 + 1) * PAIR)
            bias_ref[h // 2, rows, :] = jnp.where(valid, toep * LOG2E, NEG_BIG)

    lo = _head_masks()
    pack_rows = 16
    lo_tile = lax.broadcasted_iota(jnp.int32, (pack_rows, LANES), 1) < HEAD_DIM
    one_lo = jnp.where(lo_tile, 1.0, 0.0).astype(jnp.bfloat16)
    one_hi = jnp.where(lo_tile, 0.0, 1.0).astype(jnp.bfloat16)

    def unit(hp):
        def scores(pp, slot):
            qp = zq_ref[0, _q_rows(pp), 128 * hp:128 * hp + 128]
            band, cut = _band(pp, B_PREV, B_BAND)
            k = zkv_ref[0, band, 128 * hp:128 * hp + 128]
            s = lax.dot_general(_split_heads(qp, lo), k, NT_DIMS,
                                preferred_element_type=jnp.float32)
            s = s + bias_ref[hp, :, cut:]
            s_ref[hp, slot, :, cut:] = s
            m = jnp.max(s, axis=-1, keepdims=True)
            m_ref[hp, slot] = jnp.broadcast_to(m, (2 * PAIR, LANES))

        def weights(pp, slot):
            _, cut = _band(pp, B_PREV, B_BAND)
            m = m_ref[hp, slot]
            for t in range(cut // LANES, B_BAND // LANES):
                cols = slice(t * LANES, (t + 1) * LANES)
                e = jnp.exp2(s_ref[hp, slot, :, cols] - m).astype(jnp.bfloat16)
                p_ref[hp, slot, :, cols] = e[0:PAIR]
                p_ref[hp, slot, :, B_BAND + t * LANES:B_BAND + (t + 1) * LANES] = e[PAIR:2 * PAIR]

        def output(pp, slot):
            rows = _q_rows(pp)
            band, cut = _band(pp, B_PREV, B_BAND)
            v_h0 = zkv_ref[0, band, 512 + 128 * hp:512 + 128 * hp + 128]
            v_h1 = zkv_ref[0, band, 1024 + 128 * hp:1024 + 128 * hp + 128]
            e = jnp.concatenate([p_ref[hp, slot, :, cut:B_BAND], p_ref[hp, slot, :, B_BAND + cut:]],
                                axis=1)
            n_tiles = (B_BAND - cut) // pack_rows
            rhs = jnp.concatenate([
                jnp.concatenate([v_h0, jnp.concatenate([one_lo] * n_tiles, axis=0)], axis=1),
                jnp.concatenate([v_h1, jnp.concatenate([one_hi] * n_tiles, axis=0)], axis=1),
            ], axis=0)
            o = jnp.dot(e, rhs, preferred_element_type=jnp.float32)
            gate = zq_ref[0, rows, 512 + 128 * hp:512 + 128 * hp + 128].astype(jnp.float32)
            y_ref[0, rows, 128 * hp:128 * hp + 128] = (
                o[:, :LANES] / o[:, LANES:] * gate).astype(y_ref.dtype)

        return scores, weights, output

    _software_pipeline(n_pairs, [unit(hp) for hp in range(B_HEADS // 2)])


def _toeplitz_base_rows(rel_bias_b):
    tbl = rel_bias_b.astype(jnp.float32)
    h = tbl.shape[0]
    far = tbl[:, 2 * MAX_REL:2 * MAX_REL + 1]
    n_far = B_PREV * CHUNK - MAX_REL
    n_mid = B_BAND - n_far
    assert n_mid <= 2 * MAX_REL
    mid = tbl[:, 2 * MAX_REL:2 * MAX_REL - n_mid:-1]
    return jnp.concatenate(
        [jnp.broadcast_to(far, (h, n_far)), mid, jnp.broadcast_to(far, (h, TOEP_W - B_BAND))], axis=1)


def _attention_b(zq, zkv, b0):
    b, s, _ = zq.shape
    slots = PIPE_UNROLL
    units = B_HEADS // 2
    return pl.pallas_call(
        _attn_b_kernel,
        grid=(b,),
        in_specs=[
            pl.BlockSpec((1, s, ZQ_W), lambda bi: (bi, 0, 0)),
            pl.BlockSpec((1, s, ZKVB_W), lambda bi: (bi, 0, 0)),
            pl.BlockSpec(b0.shape, lambda bi: (0, 0)),
        ],
        out_specs=pl.BlockSpec((1, s, B_W), lambda bi: (bi, 0, 0)),
        out_shape=jax.ShapeDtypeStruct((b, s, B_W), jnp.bfloat16),
        scratch_shapes=[
            pltpu.VMEM((units, 2 * PAIR, B_BAND), jnp.float32),
            pltpu.VMEM((units, slots, 2 * PAIR, B_BAND), jnp.float32),
            pltpu.VMEM((units, slots, 2 * PAIR, LANES), jnp.float32),
            pltpu.VMEM((units, slots, PAIR, 2 * B_BAND), jnp.bfloat16),
        ],
        compiler_params=pltpu.CompilerParams(
            dimension_semantics=("arbitrary",),
            vmem_limit_bytes=VMEM_LIMIT),
        name="attention_b",
    )(zq, zkv, b0)


def _out_proj_kernel(x_ref, ya_ref, yb_ref, p_ref, w_out_ref, ple_g_ref, w_gate_ref, w_proj_ref,
                     fin_g_ref, o_ref):
    for r0 in range(0, TOK_BLOCK, OUT_SUB_BLOCK):
        rows = slice(r0, r0 + OUT_SUB_BLOCK)
        wa = ya_ref.shape[1]
        h = (x_ref[rows, :]
             + jnp.dot(ya_ref[rows, :], w_out_ref[:wa, :], preferred_element_type=jnp.float32)
             + jnp.dot(yb_ref[rows, :], w_out_ref[wa:, :], preferred_element_type=jnp.float32))
        u = (_rms_scale(h) * ple_g_ref[...]).astype(jnp.bfloat16)
        gl = jnp.dot(u, w_gate_ref[...], preferred_element_type=jnp.float32)
        gate = 1.0 / (1.0 + jnp.exp(-gl))
        pe = jnp.dot(p_ref[rows, :].astype(jnp.bfloat16), w_proj_ref[...],
                     preferred_element_type=jnp.float32)
        h = h + pe * gate
        o_ref[rows, :] = _rms_scale(h) * fin_g_ref[...]


def _out_proj(x2, ya2, yb2, p2, w_out, ple_g, w_gate, w_proj, fin_g):
    n, d = x2.shape
    pd = p2.shape[1]
    tok = lambda i: (i, 0)
    const = lambda i: (0, 0)
    return pl.pallas_call(
        _out_proj_kernel,
        grid=(n // TOK_BLOCK,),
        in_specs=[
            pl.BlockSpec((TOK_BLOCK, d), tok),
            pl.BlockSpec((TOK_BLOCK, ya2.shape[1]), tok),
            pl.BlockSpec((TOK_BLOCK, yb2.shape[1]), tok),
            pl.BlockSpec((TOK_BLOCK, pd), tok),
            pl.BlockSpec((d, d), const),
            pl.BlockSpec((1, d), const),
            pl.BlockSpec((d, d), const),
            pl.BlockSpec((pd, d), const),
            pl.BlockSpec((1, d), const),
        ],
        out_specs=pl.BlockSpec((TOK_BLOCK, d), tok),
        out_shape=jax.ShapeDtypeStruct((n, d), jnp.float32),
        compiler_params=pltpu.CompilerParams(
            dimension_semantics=("arbitrary",),
            vmem_limit_bytes=VMEM_LIMIT),
        name="out_proj",
    )(x2, ya2, yb2, p2, w_out, ple_g, w_gate, w_proj, fin_g)


@jax.jit
def kernel(x, p, norm_g, w_in, sink_a, rel_bias_b, w_out, ple_norm_g, w_ple_proj, w_ple_gate,
           final_norm_g):
    b, s, d = x.shape
    bf = jnp.bfloat16
    assert norm_g.shape[0] == 1, "the final RMSNorm is fused into the (single) layer's output kernel"
    zqa, zqb, zkva, zkvb = _in_proj(x, norm_g[0][None], w_in[0].astype(bf))
    sink = jnp.repeat(sink_a[0].astype(jnp.float32) * LOG2E, PAIR)
    sink = jnp.broadcast_to(sink.reshape(A_KV_HEADS, A_GROUP * PAIR, 1),
                            (A_KV_HEADS, A_GROUP * PAIR, LANES))
    ya = _attention_a(zqa, zkva, sink)
    yb = _attention_b(zqb, zkvb, _toeplitz_base_rows(rel_bias_b[0]))
    out = _out_proj(x.reshape(b * s, d), ya.reshape(b * s, -1), yb.reshape(b * s, -1),
                    p[0].reshape(b * s, -1), w_out[0].astype(bf), ple_norm_g[0][None],
                    w_ple_gate[0].astype(bf), w_ple_proj[0].astype(bf), final_norm_g[None])
    return out.reshape(b, s, d)
```

```python
import math

import jax
import jax.numpy as jnp
from jax import lax
from jax.experimental import pallas as pl
from jax.experimental.pallas import tpu as pltpu

D_MODEL = 1024
CHUNK = 64
HEAD_DIM = 64
A_HEADS = 8
A_KV_HEADS = 2
A_GROUP = A_HEADS // A_KV_HEADS
B_HEADS = 8
A_PREV = 2
B_PREV = 8
MAX_REL = 128
RMS_EPS = 1e-6
NEG_BIG = -1e30
LOG2E = math.log2(math.e)

A_Q = A_HEADS * HEAD_DIM
A_KV = A_KV_HEADS * HEAD_DIM
A_W = 512
B_W = 512
OFF_QA, OFF_KA, OFF_VA, OFF_GA = 0, 512, 640, 768
OFF_QB, OFF_KB, OFF_VB, OFF_GB = 1280, 1792, 2304, 2816
D_IN_PROJ = 3328

LANES = 128
PAIR = 2 * CHUNK
A_BAND = (A_PREV + 2) * CHUNK
B_BAND = (B_PREV + 2) * CHUNK
TOK_BLOCK = 1024
IN_SUB_BLOCK = 256
OUT_SUB_BLOCK = 256
TOEP_W = B_BAND + PAIR
ZQ_W = 1024
ZKVA_W = 768
ZKVB_W = 1536
PIPE_UNROLL = 4
PIPE_LAG = 2
VMEM_LIMIT = 56 * 1024 * 1024
Q_SCALE = HEAD_DIM ** -0.5 * LOG2E
NT_DIMS = (((1,), (1,)), ((), ()))


def _rms_scale(xf):
    var = jnp.mean(xf * xf, axis=-1, keepdims=True)
    return xf * lax.rsqrt(var + RMS_EPS)


def _silu(g):
    return g * (1.0 / (1.0 + jnp.exp(-g)))


def _dup_halves(v):
    rolled = pltpu.roll(v, HEAD_DIM, axis=1)
    lane = lax.broadcasted_iota(jnp.int32, v.shape, 1)
    lo = lane < HEAD_DIM
    return jnp.where(lo, v, rolled), jnp.where(lo, rolled, v)


def _in_proj_kernel(x_ref, g_ref, w_ref, zqa_ref, zqb_ref, zkva_ref, zkvb_ref):
    def project_rows(rows):
        u = (_rms_scale(x_ref[0, rows, :]) * g_ref[...]).astype(jnp.bfloat16)

        def proj(off, width):
            return jnp.dot(u, w_ref[:, off:off + width], preferred_element_type=jnp.float32)

        bf = jnp.bfloat16
        zqa_ref[0, rows, 0:512] = (proj(OFF_QA, A_Q) * Q_SCALE).astype(bf)
        zqa_ref[0, rows, 512:1024] = _silu(proj(OFF_GA, A_W)).astype(bf)
        zqb_ref[0, rows, 0:512] = (proj(OFF_QB, B_W) * Q_SCALE).astype(bf)
        zqb_ref[0, rows, 512:1024] = _silu(proj(OFF_GB, B_W)).astype(bf)
        assert OFF_VA == OFF_KA + A_KV
        kv = proj(OFF_KA, 2 * A_KV)
        kd0, kd1 = _dup_halves(kv[:, :A_KV])
        vd0, vd1 = _dup_halves(kv[:, A_KV:])
        first = lax.broadcasted_iota(jnp.int32, vd0.shape, 1) < HEAD_DIM
        zkva_ref[0, rows, 0:128] = kd0.astype(bf)
        zkva_ref[0, rows, 128:256] = kd1.astype(bf)
        for g, vd in enumerate((vd0, vd1)):
            zkva_ref[0, rows, 256 + 256 * g:384 + 256 * g] = jnp.where(first, vd, 0.0).astype(bf)
            zkva_ref[0, rows, 384 + 256 * g:512 + 256 * g] = jnp.where(first, 0.0, vd).astype(bf)
        zkvb_ref[0, rows, 0:512] = proj(OFF_KB, B_W).astype(bf)
        vb = proj(OFF_VB, B_W)
        even_head = lax.broadcasted_iota(jnp.int32, vb.shape, 1) % LANES < HEAD_DIM
        zkvb_ref[0, rows, 512:1024] = jnp.where(even_head, vb, 0.0).astype(bf)
        zkvb_ref[0, rows, 1024:1536] = jnp.where(even_head, 0.0, vb).astype(bf)

    for r0 in range(0, TOK_BLOCK, IN_SUB_BLOCK):
        project_rows(slice(r0, r0 + IN_SUB_BLOCK))


def _in_proj(x, norm_g, w_in_bf16):
    b, s, d = x.shape
    nblk = s // TOK_BLOCK
    assert nblk * TOK_BLOCK == s
    tok_idx = lambda bi, si: (bi, si, 0)
    bf = jnp.bfloat16
    return pl.pallas_call(
        _in_proj_kernel,
        grid=(b, nblk),
        in_specs=[
            pl.BlockSpec((1, TOK_BLOCK, d), tok_idx),
            pl.BlockSpec((1, d), lambda bi, si: (0, 0)),
            pl.BlockSpec((d, D_IN_PROJ), lambda bi, si: (0, 0)),
        ],
        out_specs=[
            pl.BlockSpec((1, TOK_BLOCK, ZQ_W), tok_idx),
            pl.BlockSpec((1, TOK_BLOCK, ZQ_W), tok_idx),
            pl.BlockSpec((1, TOK_BLOCK, ZKVA_W), tok_idx),
            pl.BlockSpec((1, TOK_BLOCK, ZKVB_W), tok_idx),
        ],
        out_shape=[
            jax.ShapeDtypeStruct((b, s, ZQ_W), bf),
            jax.ShapeDtypeStruct((b, s, ZQ_W), bf),
            jax.ShapeDtypeStruct((b, s, ZKVA_W), bf),
            jax.ShapeDtypeStruct((b, s, ZKVB_W), bf),
        ],
        compiler_params=pltpu.CompilerParams(
            dimension_semantics=("arbitrary", "arbitrary"),
            vmem_limit_bytes=VMEM_LIMIT),
        name="in_proj",
    )(x, norm_g, w_in_bf16)


def _band_geometry(width, n_prev):
    i = lax.broadcasted_iota(jnp.int32, (PAIR, width), 0)
    jk = lax.broadcasted_iota(jnp.int32, (PAIR, width), 1)
    ci, kc = i // CHUNK, jk // CHUNK
    return i, jk, (kc >= ci) & (kc <= ci + n_prev)


def _pv_with_row_sums(e, v):
    v_ones = jnp.concatenate([v, jnp.ones(v.shape, v.dtype)], axis=1)
    o = jnp.dot(e, v_ones, preferred_element_type=jnp.float32)
    return o[:, :LANES], o[:, LANES:]


def _software_pipeline(n, units):
    u, lag = PIPE_UNROLL, PIPE_LAG
    fill = 2 * u
    assert n % u == 0 and n >= fill + u and 2 * lag <= u

    def step(base, j, stage_range):
        for first, second, third in units:
            if 2 in stage_range:
                third(base + j - 2 * lag, (j - 2 * lag) % u)
            if 0 in stage_range:
                first(base + j, j % u)
            if 1 in stage_range:
                second(base + j - lag, (j - lag) % u)

    for j in range(fill):
        step(0, j, [k for k in range(3) if j >= k * lag])

    def steady(t, carry):
        for j in range(u):
            step(t * u, j, range(3))
        return carry

    lax.fori_loop(fill // u, n // u, steady, 0)
    for j in range(u, u + 2 * lag):
        step(n - u, j, [k for k in range(1, 3) if j - k * lag < u])


def _head_masks():
    lane = lax.broadcasted_iota(jnp.int32, (PAIR, LANES), 1)
    return lane < HEAD_DIM


def _head_ones_tiles():
    pack_rows = 16
    lo_tile = lax.broadcasted_iota(jnp.int32, (pack_rows, LANES), 1) < HEAD_DIM
    one_lo = jnp.where(lo_tile, 1.0, 0.0).astype(jnp.bfloat16)
    one_hi = jnp.where(lo_tile, 0.0, 1.0).astype(jnp.bfloat16)
    return pack_rows, one_lo, one_hi


def _split_heads(qp, lo):
    zero = jnp.zeros((), qp.dtype)
    return jnp.concatenate([jnp.where(lo, qp, zero), jnp.where(lo, zero, qp)], axis=0)


def _merge_heads(o, lo):
    return jnp.where(lo, o[0:PAIR], o[PAIR:2 * PAIR])


def _pair_start(pp):
    return pp * PAIR if isinstance(pp, int) else pl.multiple_of(pp * PAIR, PAIR)


def _q_rows(pp):
    return pl.ds(_pair_start(pp), PAIR)


def _band(pp, n_prev, width):
    if isinstance(pp, int):
        start = pp * PAIR - n_prev * CHUNK
        cut = max(0, -start)
        return pl.ds(start + cut, width - cut), cut
    assert (n_prev * CHUNK) % PAIR == 0
    return pl.ds(pl.multiple_of(pp * PAIR - n_prev * CHUNK, PAIR), width), 0


def _attn_a_kernel(zq_ref, zkv_ref, sink_ref, y_ref, bias_ref, s_ref, m_ref, p_ref, t_ref):
    n_pairs = zq_ref.shape[1] // PAIR

    @pl.when(pl.program_id(0) == 0)
    def _():
        i, jk, valid = _band_geometry(A_BAND, A_PREV)
        dist = jnp.abs(i - jk + A_PREV * CHUNK).astype(jnp.float32)
        for h in range(A_HEADS):
            slope = 2.0 ** (-8.0 * (h + 1) / A_HEADS) * LOG2E
            rows = slice((h % A_GROUP) * PAIR, (h % A_GROUP + 1) * PAIR)
            bias_ref[h // A_GROUP, rows, :] = jnp.where(valid, -slope * dist, NEG_BIG)

    lo = _head_masks()
    pack_rows, one_lo, one_hi = _head_ones_tiles()

    def unit(hp):
        g = hp // (A_GROUP // 2)
        bias_rows = slice((hp % (A_GROUP // 2)) * 2 * PAIR, (hp % (A_GROUP // 2) + 1) * 2 * PAIR)

        def scores(pp, slot):
            qp = zq_ref[0, _q_rows(pp), 128 * hp:128 * hp + 128]
            band, cut = _band(pp, A_PREV, A_BAND)
            k = zkv_ref[0, band, 128 * g:128 * g + 128]
            s = lax.dot_general(_split_heads(qp, lo), k, NT_DIMS,
                                preferred_element_type=jnp.float32)
            s = s + bias_ref[g, bias_rows, cut:]
            s_ref[hp, slot, :, cut:] = s
            m = jnp.max(s, axis=-1, keepdims=True)
            m_ref[hp, slot] = jnp.broadcast_to(m, (2 * PAIR, LANES))

        def weights(pp, slot):
            _, cut = _band(pp, A_PREV, A_BAND)
            m = m_ref[hp, slot]
            for t in range(cut // LANES, A_BAND // LANES):
                cols = slice(t * LANES, (t + 1) * LANES)
                e = jnp.exp2(s_ref[hp, slot, :, cols] - m).astype(jnp.bfloat16)
                p_ref[hp, slot, :, cols] = e[0:PAIR]
                p_ref[hp, slot, :, A_BAND + t * LANES:A_BAND + (t + 1) * LANES] = e[PAIR:2 * PAIR]
            t_ref[hp, slot] = jnp.exp2(sink_ref[hp] - _merge_heads(m, lo))

        def output(pp, slot):
            rows = _q_rows(pp)
            band, cut = _band(pp, A_PREV, A_BAND)
            v_first = zkv_ref[0, band, 256 + 256 * g:256 + 256 * g + 128]
            v_second = zkv_ref[0, band, 384 + 256 * g:384 + 256 * g + 128]
            e = jnp.concatenate([p_ref[hp, slot, :, cut:A_BAND], p_ref[hp, slot, :, A_BAND + cut:]],
                                axis=1)
            n_tiles = (A_BAND - cut) // pack_rows
            rhs = jnp.concatenate([
                jnp.concatenate([v_first, jnp.concatenate([one_lo] * n_tiles, axis=0)], axis=1),
                jnp.concatenate([v_second, jnp.concatenate([one_hi] * n_tiles, axis=0)], axis=1),
            ], axis=0)
            o = jnp.dot(e, rhs, preferred_element_type=jnp.float32)
            gate = zq_ref[0, rows, 512 + 128 * hp:512 + 128 * hp + 128].astype(jnp.float32)
            y_ref[0, rows, 128 * hp:128 * hp + 128] = (
                o[:, :LANES] / (o[:, LANES:] + t_ref[hp, slot]) * gate).astype(y_ref.dtype)

        return scores, weights, output

    _software_pipeline(n_pairs, [unit(hp) for hp in range(A_HEADS // 2)])


def _attention_a(zq, zkv, sink):
    b, s, _ = zq.shape
    slots = PIPE_UNROLL
    rows = A_GROUP * PAIR
    units = A_HEADS // 2
    return pl.pallas_call(
        _attn_a_kernel,
        grid=(b,),
        in_specs=[
            pl.BlockSpec((1, s, ZQ_W), lambda bi: (bi, 0, 0)),
            pl.BlockSpec((1, s, ZKVA_W), lambda bi: (bi, 0, 0)),
            pl.BlockSpec(sink.shape, lambda bi: (0, 0, 0)),
        ],
        out_specs=pl.BlockSpec((1, s, A_W), lambda bi: (bi, 0, 0)),
        out_shape=jax.ShapeDtypeStruct((b, s, A_W), jnp.bfloat16),
        scratch_shapes=[
            pltpu.VMEM((A_KV_HEADS, rows, A_BAND), jnp.float32),
            pltpu.VMEM((units, slots, 2 * PAIR, A_BAND), jnp.float32),
            pltpu.VMEM((units, slots, 2 * PAIR, LANES), jnp.float32),
            pltpu.VMEM((units, slots, PAIR, 2 * A_BAND), jnp.bfloat16),
            pltpu.VMEM((units, slots, PAIR, LANES), jnp.float32),
        ],
        compiler_params=pltpu.CompilerParams(
            dimension_semantics=("arbitrary",),
            vmem_limit_bytes=VMEM_LIMIT),
        name="attention_a",
    )(zq, zkv, sink)


def _attn_b_kernel(zq_ref, zkv_ref, b0_ref, y_ref, bias_ref, s_ref, m_ref, p_ref):
    n_pairs = zq_ref.shape[1] // PAIR

    @pl.when(pl.program_id(0) == 0)
    def _():
        _, _, valid = _band_geometry(B_BAND, B_PREV)
        for h in range(B_HEADS):
            row = jnp.broadcast_to(b0_ref[h:h + 1, :], (PAIR, TOEP_W))
            toep = pltpu.roll(row, 0, axis=1, stride=1, stride_axis=0)[:, :B_BAND]
            rows = slice((h % 2) * PAIR, (h % 2 + 1) * PAIR)
            bias_ref[h // 2, rows, :] = jnp.where(valid, toep * LOG2E, NEG_BIG)

    lo = _head_masks()
    pack_rows, one_lo, one_hi = _head_ones_tiles()

    def unit(hp):
        def scores(pp, slot):
            qp = zq_ref[0, _q_rows(pp), 128 * hp:128 * hp + 128]
            band, cut = _band(pp, B_PREV, B_BAND)
            k = zkv_ref[0, band, 128 * hp:128 * hp + 128]
            s = lax.dot_general(_split_heads(qp, lo), k, NT_DIMS,
                                preferred_element_type=jnp.float32)
            s = s + bias_ref[hp, :, cut:]
            s_ref[hp, slot, :, cut:] = s
            m = jnp.max(s, axis=-1, keepdims=True)
            m_ref[hp, slot] = jnp.broadcast_to(m, (2 * PAIR, LANES))

        def weights(pp, slot):
            _, cut = _band(pp, B_PREV, B_BAND)
            m = m_ref[hp, slot]
            for t in range(cut // LANES, B_BAND // LANES):
                cols = slice(t * LANES, (t + 1) * LANES)
                e = jnp.exp2(s_ref[hp, slot, :, cols] - m).astype(jnp.bfloat16)
                p_ref[hp, slot, :, cols] = e[0:PAIR]
                p_ref[hp, slot, :, B_BAND + t * LANES:B_BAND + (t + 1) * LANES] = e[PAIR:2 * PAIR]

        def output(pp, slot):
            rows = _q_rows(pp)
            band, cut = _band(pp, B_PREV, B_BAND)
            v_h0 = zkv_ref[0, band, 512 + 128 * hp:512 + 128 * hp + 128]
            v_h1 = zkv_ref[0, band, 1024 + 128 * hp:1024 + 128 * hp + 128]
            e = jnp.concatenate([p_ref[hp, slot, :, cut:B_BAND], p_ref[hp, slot, :, B_BAND + cut:]],
                                axis=1)
            n_tiles = (B_BAND - cut) // pack_rows
            rhs = jnp.concatenate([
                jnp.concatenate([v_h0, jnp.concatenate([one_lo] * n_tiles, axis=0)], axis=1),
                jnp.concatenate([v_h1, jnp.concatenate([one_hi] * n_tiles, axis=0)], axis=1),
            ], axis=0)
            o = jnp.dot(e, rhs, preferred_element_type=jnp.float32)
            gate = zq_ref[0, rows, 512 + 128 * hp:512 + 128 * hp + 128].astype(jnp.float32)
            y_ref[0, rows, 128 * hp:128 * hp + 128] = (
                o[:, :LANES] / o[:, LANES:] * gate).astype(y_ref.dtype)

        return scores, weights, output

    _software_pipeline(n_pairs, [unit(hp) for hp in range(B_HEADS // 2)])


def _toeplitz_base_rows(rel_bias_b):
    tbl = rel_bias_b.astype(jnp.float32)
    h = tbl.shape[0]
    far = tbl[:, 2 * MAX_REL:2 * MAX_REL + 1]
    n_far = B_PREV * CHUNK - MAX_REL
    n_mid = B_BAND - n_far
    assert n_mid <= 2 * MAX_REL
    mid = tbl[:, 2 * MAX_REL:2 * MAX_REL - n_mid:-1]
    return jnp.concatenate(
        [jnp.broadcast_to(far, (h, n_far)), mid, jnp.broadcast_to(far, (h, TOEP_W - B_BAND))], axis=1)


def _attention_b(zq, zkv, b0):
    b, s, _ = zq.shape
    slots = PIPE_UNROLL
    units = B_HEADS // 2
    return pl.pallas_call(
        _attn_b_kernel,
        grid=(b,),
        in_specs=[
            pl.BlockSpec((1, s, ZQ_W), lambda bi: (bi, 0, 0)),
            pl.BlockSpec((1, s, ZKVB_W), lambda bi: (bi, 0, 0)),
            pl.BlockSpec(b0.shape, lambda bi: (0, 0)),
        ],
        out_specs=pl.BlockSpec((1, s, B_W), lambda bi: (bi, 0, 0)),
        out_shape=jax.ShapeDtypeStruct((b, s, B_W), jnp.bfloat16),
        scratch_shapes=[
            pltpu.VMEM((units, 2 * PAIR, B_BAND), jnp.float32),
            pltpu.VMEM((units, slots, 2 * PAIR, B_BAND), jnp.float32),
            pltpu.VMEM((units, slots, 2 * PAIR, LANES), jnp.float32),
            pltpu.VMEM((units, slots, PAIR, 2 * B_BAND), jnp.bfloat16),
        ],
        compiler_params=pltpu.CompilerParams(
            dimension_semantics=("arbitrary",),
            vmem_limit_bytes=VMEM_LIMIT),
        name="attention_b",
    )(zq, zkv, b0)


def _out_proj_kernel(x_ref, ya_ref, yb_ref, p_ref, w_out_ref, ple_g_ref, w_gate_ref, w_proj_ref,
                     fin_g_ref, o_ref, u_ref):
    wa = ya_ref.shape[1]

    def residual(rows):
        h = (x_ref[rows, :]
             + jnp.dot(ya_ref[rows, :], w_out_ref[:wa, :], preferred_element_type=jnp.float32)
             + jnp.dot(yb_ref[rows, :], w_out_ref[wa:, :], preferred_element_type=jnp.float32))
        o_ref[rows, :] = h
        u_ref[rows, :] = (_rms_scale(h) * ple_g_ref[...]).astype(jnp.bfloat16)

    def embed(rows):
        gl = jnp.dot(u_ref[rows, :], w_gate_ref[...], preferred_element_type=jnp.float32)
        gate = 1.0 / (1.0 + jnp.exp(-gl))
        pe = jnp.dot(p_ref[rows, :].astype(jnp.bfloat16), w_proj_ref[...],
                     preferred_element_type=jnp.float32)
        h = o_ref[rows, :] + pe * gate
        o_ref[rows, :] = _rms_scale(h) * fin_g_ref[...]

    subs = [slice(r0, r0 + OUT_SUB_BLOCK) for r0 in range(0, TOK_BLOCK, OUT_SUB_BLOCK)]
    lead = 2
    for i in range(lead):
        residual(subs[i])
    for i in range(len(subs)):
        if i + lead < len(subs):
            residual(subs[i + lead])
        embed(subs[i])


def _out_proj(x2, ya2, yb2, p2, w_out, ple_g, w_gate, w_proj, fin_g):
    n, d = x2.shape
    pd = p2.shape[1]
    tok = lambda i: (i, 0)
    const = lambda i: (0, 0)
    return pl.pallas_call(
        _out_proj_kernel,
        grid=(n // TOK_BLOCK,),
        in_specs=[
            pl.BlockSpec((TOK_BLOCK, d), tok),
            pl.BlockSpec((TOK_BLOCK, ya2.shape[1]), tok),
            pl.BlockSpec((TOK_BLOCK, yb2.shape[1]), tok),
            pl.BlockSpec((TOK_BLOCK, pd), tok),
            pl.BlockSpec((d, d), const),
            pl.BlockSpec((1, d), const),
            pl.BlockSpec((d, d), const),
            pl.BlockSpec((pd, d), const),
            pl.BlockSpec((1, d), const),
        ],
        out_specs=pl.BlockSpec((TOK_BLOCK, d), tok),
        out_shape=jax.ShapeDtypeStruct((n, d), jnp.float32),
        scratch_shapes=[pltpu.VMEM((TOK_BLOCK, d), jnp.bfloat16)],
        compiler_params=pltpu.CompilerParams(
            dimension_semantics=("arbitrary",),
            vmem_limit_bytes=VMEM_LIMIT),
        name="out_proj",
    )(x2, ya2, yb2, p2, w_out, ple_g, w_gate, w_proj, fin_g)


@jax.jit
def kernel(x, p, norm_g, w_in, sink_a, rel_bias_b, w_out, ple_norm_g, w_ple_proj, w_ple_gate,
           final_norm_g):
    b, s, d = x.shape
    bf = jnp.bfloat16
    assert norm_g.shape[0] == 1, "the final RMSNorm is fused into the (single) layer's output kernel"
    zqa, zqb, zkva, zkvb = _in_proj(x, norm_g[0][None], w_in[0].astype(bf))
    sink = (sink_a[0].astype(jnp.float32) * LOG2E).reshape(A_HEADS // 2, 1, 2, 1)
    sink = jnp.broadcast_to(sink, (A_HEADS // 2, PAIR, 2, HEAD_DIM)).reshape(A_HEADS // 2, PAIR, LANES)
    ya = _attention_a(zqa, zkva, sink)
    yb = _attention_b(zqb, zkvb, _toeplitz_base_rows(rel_bias_b[0]))
    out = _out_proj(x.reshape(b * s, d), ya.reshape(b * s, -1), yb.reshape(b * s, -1),
                    p[0].reshape(b * s, -1), w_out[0].astype(bf), ple_norm_g[0][None],
                    w_ple_gate[0].astype(bf), w_ple_proj[0].astype(bf), final_norm_g[None])
    return out.reshape(b, s, d)
```

```python
import math

import jax
import jax.numpy as jnp
from jax import lax
from jax.experimental import pallas as pl
from jax.experimental.pallas import tpu as pltpu

D_MODEL = 1024
CHUNK = 64
HEAD_DIM = 64
A_HEADS = 8
A_KV_HEADS = 2
A_GROUP = A_HEADS // A_KV_HEADS
B_HEADS = 8
A_PREV = 2
B_PREV = 8
MAX_REL = 128
RMS_EPS = 1e-6
NEG_BIG = -1e30
LOG2E = math.log2(math.e)

A_Q = A_HEADS * HEAD_DIM
A_KV = A_KV_HEADS * HEAD_DIM
A_W = 512
B_W = 512
OFF_QA, OFF_KA, OFF_VA, OFF_GA = 0, 512, 640, 768
OFF_QB, OFF_KB, OFF_VB, OFF_GB = 1280, 1792, 2304, 2816
D_IN_PROJ = 3328

LANES = 128
PAIR = 2 * CHUNK
A_BAND = (A_PREV + 2) * CHUNK
B_BAND = (B_PREV + 2) * CHUNK
TOK_BLOCK = 1024
IN_SUB_BLOCK = 256
OUT_SUB_BLOCK = 256
TOEP_W = B_BAND + PAIR
ZQ_W = 1024
ZKVA_W = 768
ZKVB_W = 1536
PIPE_UNROLL = 4
A_PIPE_LAG = 1
B_PIPE_LAG = 2
VMEM_LIMIT = 56 * 1024 * 1024
Q_SCALE = HEAD_DIM ** -0.5 * LOG2E
NT_DIMS = (((1,), (1,)), ((), ()))


def _rms_scale(xf):
    var = jnp.mean(xf * xf, axis=-1, keepdims=True)
    return xf * lax.rsqrt(var + RMS_EPS)


def _silu(g):
    return g * (1.0 / (1.0 + jnp.exp(-g)))


def _dup_halves(v):
    rolled = pltpu.roll(v, HEAD_DIM, axis=1)
    lane = lax.broadcasted_iota(jnp.int32, v.shape, 1)
    lo = lane < HEAD_DIM
    return jnp.where(lo, v, rolled), jnp.where(lo, rolled, v)


def _in_proj_kernel(x_ref, g_ref, w_ref, zqa_ref, zqb_ref, zkva_ref, zkvb_ref):
    def project_rows(rows):
        u = (_rms_scale(x_ref[0, rows, :]) * g_ref[...]).astype(jnp.bfloat16)

        def proj(off, width):
            return jnp.dot(u, w_ref[:, off:off + width], preferred_element_type=jnp.float32)

        bf = jnp.bfloat16
        zqa_ref[0, rows, 0:512] = (proj(OFF_QA, A_Q) * Q_SCALE).astype(bf)
        zqa_ref[0, rows, 512:1024] = _silu(proj(OFF_GA, A_W)).astype(bf)
        zqb_ref[0, rows, 0:512] = (proj(OFF_QB, B_W) * Q_SCALE).astype(bf)
        zqb_ref[0, rows, 512:1024] = _silu(proj(OFF_GB, B_W)).astype(bf)
        assert OFF_VA == OFF_KA + A_KV
        kv = proj(OFF_KA, 2 * A_KV)
        kd0, kd1 = _dup_halves(kv[:, :A_KV])
        vd0, vd1 = _dup_halves(kv[:, A_KV:])
        first = lax.broadcasted_iota(jnp.int32, vd0.shape, 1) < HEAD_DIM
        zkva_ref[0, rows, 0:128] = kd0.astype(bf)
        zkva_ref[0, rows, 128:256] = kd1.astype(bf)
        for g, vd in enumerate((vd0, vd1)):
            zkva_ref[0, rows, 256 + 256 * g:384 + 256 * g] = jnp.where(first, vd, 0.0).astype(bf)
            zkva_ref[0, rows, 384 + 256 * g:512 + 256 * g] = jnp.where(first, 0.0, vd).astype(bf)
        zkvb_ref[0, rows, 0:512] = proj(OFF_KB, B_W).astype(bf)
        vb = proj(OFF_VB, B_W)
        even_head = lax.broadcasted_iota(jnp.int32, vb.shape, 1) % LANES < HEAD_DIM
        zkvb_ref[0, rows, 512:1024] = jnp.where(even_head, vb, 0.0).astype(bf)
        zkvb_ref[0, rows, 1024:1536] = jnp.where(even_head, 0.0, vb).astype(bf)

    for r0 in range(0, TOK_BLOCK, IN_SUB_BLOCK):
        project_rows(slice(r0, r0 + IN_SUB_BLOCK))


def _in_proj(x, norm_g, w_in_bf16):
    b, s, d = x.shape
    nblk = s // TOK_BLOCK
    assert nblk * TOK_BLOCK == s
    tok_idx = lambda bi, si: (bi, si, 0)
    bf = jnp.bfloat16
    return pl.pallas_call(
        _in_proj_kernel,
        grid=(b, nblk),
        in_specs=[
            pl.BlockSpec((1, TOK_BLOCK, d), tok_idx),
            pl.BlockSpec((1, d), lambda bi, si: (0, 0)),
            pl.BlockSpec((d, D_IN_PROJ), lambda bi, si: (0, 0)),
        ],
        out_specs=[
            pl.BlockSpec((1, TOK_BLOCK, ZQ_W), tok_idx),
            pl.BlockSpec((1, TOK_BLOCK, ZQ_W), tok_idx),
            pl.BlockSpec((1, TOK_BLOCK, ZKVA_W), tok_idx),
            pl.BlockSpec((1, TOK_BLOCK, ZKVB_W), tok_idx),
        ],
        out_shape=[
            jax.ShapeDtypeStruct((b, s, ZQ_W), bf),
            jax.ShapeDtypeStruct((b, s, ZQ_W), bf),
            jax.ShapeDtypeStruct((b, s, ZKVA_W), bf),
            jax.ShapeDtypeStruct((b, s, ZKVB_W), bf),
        ],
        compiler_params=pltpu.CompilerParams(
            dimension_semantics=("arbitrary", "arbitrary"),
            vmem_limit_bytes=VMEM_LIMIT),
        name="in_proj",
    )(x, norm_g, w_in_bf16)


def _band_geometry(width, n_prev):
    i = lax.broadcasted_iota(jnp.int32, (PAIR, width), 0)
    jk = lax.broadcasted_iota(jnp.int32, (PAIR, width), 1)
    ci, kc = i // CHUNK, jk // CHUNK
    return i, jk, (kc >= ci) & (kc <= ci + n_prev)


def _pv_with_row_sums(e, v):
    v_ones = jnp.concatenate([v, jnp.ones(v.shape, v.dtype)], axis=1)
    o = jnp.dot(e, v_ones, preferred_element_type=jnp.float32)
    return o[:, :LANES], o[:, LANES:]


def _software_pipeline(n, units, lag, n_cut, min_fill=0):
    u = PIPE_UNROLL
    fill = max(-(-(n_cut + 2 * lag) // u) * u, min_fill)
    assert n % u == 0 and n >= fill and 2 * lag <= u

    def step(base, j, stage_range):
        for first, second, third in units:
            if 2 in stage_range:
                third(base + j - 2 * lag, (j - 2 * lag) % u)
            if 0 in stage_range:
                first(base + j, j % u)
            if 1 in stage_range:
                second(base + j - lag, (j - lag) % u)

    for j in range(fill):
        step(0, j, [k for k in range(3) if j >= k * lag])

    def steady(t, carry):
        for j in range(u):
            step(t * u, j, range(3))
        return carry

    if fill < n:
        lax.fori_loop(fill // u, n // u, steady, 0)
    for j in range(u, u + 2 * lag):
        step(n - u, j, [k for k in range(1, 3) if j - k * lag < u])


def _head_masks():
    lane = lax.broadcasted_iota(jnp.int32, (PAIR, LANES), 1)
    return lane < HEAD_DIM


def _head_ones_tiles():
    pack_rows = 16
    lo_tile = lax.broadcasted_iota(jnp.int32, (pack_rows, LANES), 1) < HEAD_DIM
    one_lo = jnp.where(lo_tile, 1.0, 0.0).astype(jnp.bfloat16)
    one_hi = jnp.where(lo_tile, 0.0, 1.0).astype(jnp.bfloat16)
    return pack_rows, one_lo, one_hi


def _split_heads(qp, lo):
    zero = jnp.zeros((), qp.dtype)
    return jnp.concatenate([jnp.where(lo, qp, zero), jnp.where(lo, zero, qp)], axis=0)


def _merge_heads(o, lo):
    return jnp.where(lo, o[0:PAIR], o[PAIR:2 * PAIR])


def _pair_start(pp):
    return pp * PAIR if isinstance(pp, int) else pl.multiple_of(pp * PAIR, PAIR)


def _q_rows(pp):
    return pl.ds(_pair_start(pp), PAIR)


def _band(pp, n_prev, width):
    if isinstance(pp, int):
        start = pp * PAIR - n_prev * CHUNK
        cut = max(0, -start)
        return pl.ds(start + cut, width - cut), cut
    assert (n_prev * CHUNK) % PAIR == 0
    return pl.ds(pl.multiple_of(pp * PAIR - n_prev * CHUNK, PAIR), width), 0


def _attn_a_kernel(zq_ref, zkv_ref, sink_ref, y_ref, bias_ref, s_ref, m_ref, p_ref, t_ref):
    n_pairs = zq_ref.shape[1] // PAIR

    @pl.when(pl.program_id(0) == 0)
    def _():
        i, jk, valid = _band_geometry(A_BAND, A_PREV)
        dist = jnp.abs(i - jk + A_PREV * CHUNK).astype(jnp.float32)
        for h in range(A_HEADS):
            slope = 2.0 ** (-8.0 * (h + 1) / A_HEADS) * LOG2E
            rows = slice((h % A_GROUP) * PAIR, (h % A_GROUP + 1) * PAIR)
            bias_ref[h // A_GROUP, rows, :] = jnp.where(valid, -slope * dist, NEG_BIG)

    lo = _head_masks()
    pack_rows, one_lo, one_hi = _head_ones_tiles()

    def unit(hp):
        g = hp // (A_GROUP // 2)
        bias_rows = slice((hp % (A_GROUP // 2)) * 2 * PAIR, (hp % (A_GROUP // 2) + 1) * 2 * PAIR)

        def scores(pp, slot):
            qp = zq_ref[0, _q_rows(pp), 128 * hp:128 * hp + 128]
            band, cut = _band(pp, A_PREV, A_BAND)
            k = zkv_ref[0, band, 128 * g:128 * g + 128]
            s = lax.dot_general(_split_heads(qp, lo), k, NT_DIMS,
                                preferred_element_type=jnp.float32)
            s = s + bias_ref[g, bias_rows, cut:]
            s_ref[hp, slot, :, cut:] = s
            m = jnp.max(s, axis=-1, keepdims=True)
            m_ref[hp, slot] = jnp.broadcast_to(m, (2 * PAIR, LANES))

        def weights(pp, slot):
            _, cut = _band(pp, A_PREV, A_BAND)
            m = m_ref[hp, slot]
            for t in range(cut // LANES, A_BAND // LANES):
                cols = slice(t * LANES, (t + 1) * LANES)
                e = jnp.exp2(s_ref[hp, slot, :, cols] - m).astype(jnp.bfloat16)
                p_ref[hp, slot, :, cols] = e[0:PAIR]
                p_ref[hp, slot, :, A_BAND + t * LANES:A_BAND + (t + 1) * LANES] = e[PAIR:2 * PAIR]
            t_ref[hp, slot] = jnp.exp2(sink_ref[hp] - _merge_heads(m, lo))

        def output(pp, slot):
            rows = _q_rows(pp)
            band, cut = _band(pp, A_PREV, A_BAND)
            v_first = zkv_ref[0, band, 256 + 256 * g:256 + 256 * g + 128]
            v_second = zkv_ref[0, band, 384 + 256 * g:384 + 256 * g + 128]
            e = jnp.concatenate([p_ref[hp, slot, :, cut:A_BAND], p_ref[hp, slot, :, A_BAND + cut:]],
                                axis=1)
            n_tiles = (A_BAND - cut) // pack_rows
            rhs = jnp.concatenate([
                jnp.concatenate([v_first, jnp.concatenate([one_lo] * n_tiles, axis=0)], axis=1),
                jnp.concatenate([v_second, jnp.concatenate([one_hi] * n_tiles, axis=0)], axis=1),
            ], axis=0)
            o = jnp.dot(e, rhs, preferred_element_type=jnp.float32)
            gate = zq_ref[0, rows, 512 + 128 * hp:512 + 128 * hp + 128].astype(jnp.float32)
            y_ref[0, rows, 128 * hp:128 * hp + 128] = (
                o[:, :LANES] / (o[:, LANES:] + t_ref[hp, slot]) * gate).astype(y_ref.dtype)

        return scores, weights, output

    _software_pipeline(n_pairs, [unit(hp) for hp in range(A_HEADS // 2)], A_PIPE_LAG,
                       -(-A_PREV * CHUNK // PAIR), min_fill=n_pairs)


def _attention_a(zq, zkv, sink):
    b, s, _ = zq.shape
    slots = PIPE_UNROLL
    rows = A_GROUP * PAIR
    units = A_HEADS // 2
    return pl.pallas_call(
        _attn_a_kernel,
        grid=(b,),
        in_specs=[
            pl.BlockSpec((1, s, ZQ_W), lambda bi: (bi, 0, 0)),
            pl.BlockSpec((1, s, ZKVA_W), lambda bi: (bi, 0, 0)),
            pl.BlockSpec(sink.shape, lambda bi: (0, 0, 0)),
        ],
        out_specs=pl.BlockSpec((1, s, A_W), lambda bi: (bi, 0, 0)),
        out_shape=jax.ShapeDtypeStruct((b, s, A_W), jnp.bfloat16),
        scratch_shapes=[
            pltpu.VMEM((A_KV_HEADS, rows, A_BAND), jnp.float32),
            pltpu.VMEM((units, slots, 2 * PAIR, A_BAND), jnp.float32),
            pltpu.VMEM((units, slots, 2 * PAIR, LANES), jnp.float32),
            pltpu.VMEM((units, slots, PAIR, 2 * A_BAND), jnp.bfloat16),
            pltpu.VMEM((units, slots, PAIR, LANES), jnp.float32),
        ],
        compiler_params=pltpu.CompilerParams(
            dimension_semantics=("arbitrary",),
            vmem_limit_bytes=VMEM_LIMIT),
        name="attention_a",
    )(zq, zkv, sink)


def _attn_b_kernel(zq_ref, zkv_ref, b0_ref, y_ref, bias_ref, s_ref, m_ref, p_ref):
    n_pairs = zq_ref.shape[1] // PAIR

    @pl.when(pl.program_id(0) == 0)
    def _():
        _, _, valid = _band_geometry(B_BAND, B_PREV)
        for h in range(B_HEADS):
            row = jnp.broadcast_to(b0_ref[h:h + 1, :], (PAIR, TOEP_W))
            toep = pltpu.roll(row, 0, axis=1, stride=1, stride_axis=0)[:, :B_BAND]
            rows = slice((h % 2) * PAIR, (h % 2 + 1) * PAIR)
            bias_ref[h // 2, rows, :] = jnp.where(valid, toep * LOG2E, NEG_BIG)

    lo = _head_masks()
    pack_rows, one_lo, one_hi = _head_ones_tiles()

    def unit(hp):
        def scores(pp, slot):
            qp = zq_ref[0, _q_rows(pp), 128 * hp:128 * hp + 128]
            band, cut = _band(pp, B_PREV, B_BAND)
            k = zkv_ref[0, band, 128 * hp:128 * hp + 128]
            s = lax.dot_general(_split_heads(qp, lo), k, NT_DIMS,
                                preferred_element_type=jnp.float32)
            s = s + bias_ref[hp, :, cut:]
            s_ref[hp, slot, :, cut:] = s
            m = jnp.max(s, axis=-1, keepdims=True)
            m_ref[hp, slot] = jnp.broadcast_to(m, (2 * PAIR, LANES))

        def weights(pp, slot):
            _, cut = _band(pp, B_PREV, B_BAND)
            m = m_ref[hp, slot]
            for t in range(cut // LANES, B_BAND // LANES):
                cols = slice(t * LANES, (t + 1) * LANES)
                e = jnp.exp2(s_ref[hp, slot, :, cols] - m).astype(jnp.bfloat16)
                p_ref[hp, slot, :, cols] = e[0:PAIR]
                p_ref[hp, slot, :, B_BAND + t * LANES:B_BAND + (t + 1) * LANES] = e[PAIR:2 * PAIR]

        def output(pp, slot):
            rows = _q_rows(pp)
            band, cut = _band(pp, B_PREV, B_BAND)
            v_h0 = zkv_ref[0, band, 512 + 128 * hp:512 + 128 * hp + 128]
            v_h1 = zkv_ref[0, band, 1024 + 128 * hp:1024 + 128 * hp + 128]
            e = jnp.concatenate([p_ref[hp, slot, :, cut:B_BAND], p_ref[hp, slot, :, B_BAND + cut:]],
                                axis=1)
            n_tiles = (B_BAND - cut) // pack_rows
            rhs = jnp.concatenate([
                jnp.concatenate([v_h0, jnp.concatenate([one_lo] * n_tiles, axis=0)], axis=1),
                jnp.concatenate([v_h1, jnp.concatenate([one_hi] * n_tiles, axis=0)], axis=1),
            ], axis=0)
            o = jnp.dot(e, rhs, preferred_element_type=jnp.float32)
            gate = zq_ref[0, rows, 512 + 128 * hp:512 + 128 * hp + 128].astype(jnp.float32)
            y_ref[0, rows, 128 * hp:128 * hp + 128] = (
                o[:, :LANES] / o[:, LANES:] * gate).astype(y_ref.dtype)

        return scores, weights, output

    _software_pipeline(n_pairs, [unit(hp) for hp in range(B_HEADS // 2)], B_PIPE_LAG,
                       -(-B_PREV * CHUNK // PAIR))


def _toeplitz_base_rows(rel_bias_b):
    tbl = rel_bias_b.astype(jnp.float32)
    h = tbl.shape[0]
    far = tbl[:, 2 * MAX_REL:2 * MAX_REL + 1]
    n_far = B_PREV * CHUNK - MAX_REL
    n_mid = B_BAND - n_far
    assert n_mid <= 2 * MAX_REL
    mid = tbl[:, 2 * MAX_REL:2 * MAX_REL - n_mid:-1]
    return jnp.concatenate(
        [jnp.broadcast_to(far, (h, n_far)), mid, jnp.broadcast_to(far, (h, TOEP_W - B_BAND))], axis=1)


def _attention_b(zq, zkv, b0):
    b, s, _ = zq.shape
    slots = PIPE_UNROLL
    units = B_HEADS // 2
    return pl.pallas_call(
        _attn_b_kernel,
        grid=(b,),
        in_specs=[
            pl.BlockSpec((1, s, ZQ_W), lambda bi: (bi, 0, 0)),
            pl.BlockSpec((1, s, ZKVB_W), lambda bi: (bi, 0, 0)),
            pl.BlockSpec(b0.shape, lambda bi: (0, 0)),
        ],
        out_specs=pl.BlockSpec((1, s, B_W), lambda bi: (bi, 0, 0)),
        out_shape=jax.ShapeDtypeStruct((b, s, B_W), jnp.bfloat16),
        scratch_shapes=[
            pltpu.VMEM((units, 2 * PAIR, B_BAND), jnp.float32),
            pltpu.VMEM((units, slots, 2 * PAIR, B_BAND), jnp.float32),
            pltpu.VMEM((units, slots, 2 * PAIR, LANES), jnp.float32),
            pltpu.VMEM((units, slots, PAIR, 2 * B_BAND), jnp.bfloat16),
        ],
        compiler_params=pltpu.CompilerParams(
            dimension_semantics=("arbitrary",),
            vmem_limit_bytes=VMEM_LIMIT),
        name="attention_b",
    )(zq, zkv, b0)


def _out_proj_kernel(x_ref, ya_ref, yb_ref, p_ref, w_out_ref, ple_g_ref, w_gate_ref, w_proj_ref,
                     fin_g_ref, o_ref, u_ref):
    wa = ya_ref.shape[1]

    def residual(rows):
        h = (x_ref[rows, :]
             + jnp.dot(ya_ref[rows, :], w_out_ref[:wa, :], preferred_element_type=jnp.float32)
             + jnp.dot(yb_ref[rows, :], w_out_ref[wa:, :], preferred_element_type=jnp.float32))
        o_ref[rows, :] = h
        u_ref[rows, :] = (_rms_scale(h) * ple_g_ref[...]).astype(jnp.bfloat16)

    def embed(rows):
        gl = jnp.dot(u_ref[rows, :], w_gate_ref[...], preferred_element_type=jnp.float32)
        gate = 1.0 / (1.0 + jnp.exp(-gl))
        pe = jnp.dot(p_ref[rows, :].astype(jnp.bfloat16), w_proj_ref[...],
                     preferred_element_type=jnp.float32)
        h = o_ref[rows, :] + pe * gate
        o_ref[rows, :] = _rms_scale(h) * fin_g_ref[...]

    subs = [slice(r0, r0 + OUT_SUB_BLOCK) for r0 in range(0, TOK_BLOCK, OUT_SUB_BLOCK)]
    lead = 2
    for i in range(lead):
        residual(subs[i])
    for i in range(len(subs)):
        if i + lead < len(subs):
            residual(subs[i + lead])
        embed(subs[i])


def _out_proj(x2, ya2, yb2, p2, w_out, ple_g, w_gate, w_proj, fin_g):
    n, d = x2.shape
    pd = p2.shape[1]
    tok = lambda i: (i, 0)
    const = lambda i: (0, 0)
    return pl.pallas_call(
        _out_proj_kernel,
        grid=(n // TOK_BLOCK,),
        in_specs=[
            pl.BlockSpec((TOK_BLOCK, d), tok),
            pl.BlockSpec((TOK_BLOCK, ya2.shape[1]), tok),
            pl.BlockSpec((TOK_BLOCK, yb2.shape[1]), tok),
            pl.BlockSpec((TOK_BLOCK, pd), tok),
            pl.BlockSpec((d, d), const),
            pl.BlockSpec((1, d), const),
            pl.BlockSpec((d, d), const),
            pl.BlockSpec((pd, d), const),
            pl.BlockSpec((1, d), const),
        ],
        out_specs=pl.BlockSpec((TOK_BLOCK, d), tok),
        out_shape=jax.ShapeDtypeStruct((n, d), jnp.float32),
        scratch_shapes=[pltpu.VMEM((TOK_BLOCK, d), jnp.bfloat16)],
        compiler_params=pltpu.CompilerParams(
            dimension_semantics=("arbitrary",),
            vmem_limit_bytes=VMEM_LIMIT),
        name="out_proj",
    )(x2, ya2, yb2, p2, w_out, ple_g, w_gate, w_proj, fin_g)


@jax.jit
def kernel(x, p, norm_g, w_in, sink_a, rel_bias_b, w_out, ple_norm_g, w_ple_proj, w_ple_gate,
           final_norm_g):
    b, s, d = x.shape
    bf = jnp.bfloat16
    assert norm_g.shape[0] == 1, "the final RMSNorm is fused into the (single) layer's output kernel"
    zqa, zqb, zkva, zkvb = _in_proj(x, norm_g[0][None], w_in[0].astype(bf))
    sink = (sink_a[0].astype(jnp.float32) * LOG2E).reshape(A_HEADS // 2, 1, 2, 1)
    sink = jnp.broadcast_to(sink, (A_HEADS // 2, PAIR, 2, HEAD_DIM)).reshape(A_HEADS // 2, PAIR, LANES)
    ya = _attention_a(zqa, zkva, sink)
    yb = _attention_b(zqb, zkvb, _toeplitz_base_rows(rel_bias_b[0]))
    out = _out_proj(x.reshape(b * s, d), ya.reshape(b * s, -1), yb.reshape(b * s, -1),
                    p[0].reshape(b * s, -1), w_out[0].astype(bf), ple_norm_g[0][None],
                    w_ple_gate[0].astype(bf), w_ple_proj[0].astype(bf), final_norm_g[None])
    return out.reshape(b, s, d)
```

```python
import math

import jax
import jax.numpy as jnp
from jax import lax
from jax.experimental import pallas as pl
from jax.experimental.pallas import tpu as pltpu

D_MODEL = 1024
CHUNK = 64
HEAD_DIM = 64
A_HEADS = 8
A_KV_HEADS = 2
A_GROUP = A_HEADS // A_KV_HEADS
B_HEADS = 8
A_PREV = 2
B_PREV = 8
MAX_REL = 128
RMS_EPS = 1e-6
NEG_BIG = -1e30
LOG2E = math.log2(math.e)

A_Q = A_HEADS * HEAD_DIM
A_KV = A_KV_HEADS * HEAD_DIM
A_W = 512
B_W = 512
OFF_QA, OFF_KA, OFF_VA, OFF_GA = 0, 512, 640, 768
OFF_QB, OFF_KB, OFF_VB, OFF_GB = 1280, 1792, 2304, 2816
D_IN_PROJ = 3328

LANES = 128
PAIR = 2 * CHUNK
A_BAND = (A_PREV + 2) * CHUNK
B_BAND = (B_PREV + 2) * CHUNK
TOK_BLOCK = 1024
IN_SUB_BLOCK = 256
OUT_SUB_BLOCK = 256
TOEP_W = B_BAND + PAIR
ZQ_W = 1024
ZKVA_W = 768
ZKVB_W = 1536
PIPE_UNROLL = 4
A_PIPE_LAG = 1
B_PIPE_LAG = 1
VMEM_LIMIT = 56 * 1024 * 1024
Q_SCALE = HEAD_DIM ** -0.5 * LOG2E
NT_DIMS = (((1,), (1,)), ((), ()))


def _rms_scale(xf):
    var = jnp.mean(xf * xf, axis=-1, keepdims=True)
    return xf * lax.rsqrt(var + RMS_EPS)


def _silu(g):
    return g * (1.0 / (1.0 + jnp.exp(-g)))


def _dup_halves(v):
    rolled = pltpu.roll(v, HEAD_DIM, axis=1)
    lane = lax.broadcasted_iota(jnp.int32, v.shape, 1)
    lo = lane < HEAD_DIM
    return jnp.where(lo, v, rolled), jnp.where(lo, rolled, v)


def _in_proj_kernel(x_ref, g_ref, w_ref, zqa_ref, zqb_ref, zkva_ref, zkvb_ref):
    def project_rows(rows):
        u = (_rms_scale(x_ref[0, rows, :]) * g_ref[...]).astype(jnp.bfloat16)

        def proj(off, width):
            return jnp.dot(u, w_ref[:, off:off + width], preferred_element_type=jnp.float32)

        bf = jnp.bfloat16
        zqa_ref[0, rows, 0:512] = (proj(OFF_QA, A_Q) * Q_SCALE).astype(bf)
        zqa_ref[0, rows, 512:1024] = _silu(proj(OFF_GA, A_W)).astype(bf)
        zqb_ref[0, rows, 0:512] = (proj(OFF_QB, B_W) * Q_SCALE).astype(bf)
        zqb_ref[0, rows, 512:1024] = _silu(proj(OFF_GB, B_W)).astype(bf)
        assert OFF_VA == OFF_KA + A_KV
        kv = proj(OFF_KA, 2 * A_KV)
        kd0, kd1 = _dup_halves(kv[:, :A_KV])
        vd0, vd1 = _dup_halves(kv[:, A_KV:])
        first = lax.broadcasted_iota(jnp.int32, vd0.shape, 1) < HEAD_DIM
        zkva_ref[0, rows, 0:128] = kd0.astype(bf)
        zkva_ref[0, rows, 128:256] = kd1.astype(bf)
        for g, vd in enumerate((vd0, vd1)):
            zkva_ref[0, rows, 256 + 256 * g:384 + 256 * g] = jnp.where(first, vd, 0.0).astype(bf)
            zkva_ref[0, rows, 384 + 256 * g:512 + 256 * g] = jnp.where(first, 0.0, vd).astype(bf)
        zkvb_ref[0, rows, 0:512] = proj(OFF_KB, B_W).astype(bf)
        vb = proj(OFF_VB, B_W)
        even_head = lax.broadcasted_iota(jnp.int32, vb.shape, 1) % LANES < HEAD_DIM
        zkvb_ref[0, rows, 512:1024] = jnp.where(even_head, vb, 0.0).astype(bf)
        zkvb_ref[0, rows, 1024:1536] = jnp.where(even_head, 0.0, vb).astype(bf)

    for r0 in range(0, TOK_BLOCK, IN_SUB_BLOCK):
        project_rows(slice(r0, r0 + IN_SUB_BLOCK))


def _in_proj(x, norm_g, w_in_bf16):
    b, s, d = x.shape
    nblk = s // TOK_BLOCK
    assert nblk * TOK_BLOCK == s
    tok_idx = lambda bi, si: (bi, si, 0)
    bf = jnp.bfloat16
    return pl.pallas_call(
        _in_proj_kernel,
        grid=(b, nblk),
        in_specs=[
            pl.BlockSpec((1, TOK_BLOCK, d), tok_idx),
            pl.BlockSpec((1, d), lambda bi, si: (0, 0)),
            pl.BlockSpec((d, D_IN_PROJ), lambda bi, si: (0, 0)),
        ],
        out_specs=[
            pl.BlockSpec((1, TOK_BLOCK, ZQ_W), tok_idx),
            pl.BlockSpec((1, TOK_BLOCK, ZQ_W), tok_idx),
            pl.BlockSpec((1, TOK_BLOCK, ZKVA_W), tok_idx),
            pl.BlockSpec((1, TOK_BLOCK, ZKVB_W), tok_idx),
        ],
        out_shape=[
            jax.ShapeDtypeStruct((b, s, ZQ_W), bf),
            jax.ShapeDtypeStruct((b, s, ZQ_W), bf),
            jax.ShapeDtypeStruct((b, s, ZKVA_W), bf),
            jax.ShapeDtypeStruct((b, s, ZKVB_W), bf),
        ],
        compiler_params=pltpu.CompilerParams(
            dimension_semantics=("arbitrary", "arbitrary"),
            vmem_limit_bytes=VMEM_LIMIT),
        name="in_proj",
    )(x, norm_g, w_in_bf16)


def _band_geometry(width, n_prev):
    i = lax.broadcasted_iota(jnp.int32, (PAIR, width), 0)
    jk = lax.broadcasted_iota(jnp.int32, (PAIR, width), 1)
    ci, kc = i // CHUNK, jk // CHUNK
    return i, jk, (kc >= ci) & (kc <= ci + n_prev)


def _pv_with_row_sums(e, v):
    v_ones = jnp.concatenate([v, jnp.ones(v.shape, v.dtype)], axis=1)
    o = jnp.dot(e, v_ones, preferred_element_type=jnp.float32)
    return o[:, :LANES], o[:, LANES:]


def _software_pipeline(n, units, lag, n_cut, min_fill=0):
    u = PIPE_UNROLL
    fill = max(-(-(n_cut + 2 * lag) // u) * u, min_fill)
    assert n % u == 0 and n >= fill and 2 * lag <= u

    def step(base, j, stage_range):
        for first, second, third in units:
            if 2 in stage_range:
                third(base + j - 2 * lag, (j - 2 * lag) % u)
            if 0 in stage_range:
                first(base + j, j % u)
            if 1 in stage_range:
                second(base + j - lag, (j - lag) % u)

    for j in range(fill):
        step(0, j, [k for k in range(3) if j >= k * lag])

    def steady(t, carry):
        for j in range(u):
            step(t * u, j, range(3))
        return carry

    if fill < n:
        lax.fori_loop(fill // u, n // u, steady, 0)
    for j in range(u, u + 2 * lag):
        step(n - u, j, [k for k in range(1, 3) if j - k * lag < u])


def _head_masks():
    lane = lax.broadcasted_iota(jnp.int32, (PAIR, LANES), 1)
    return lane < HEAD_DIM


def _head_ones_tiles():
    pack_rows = 16
    lo_tile = lax.broadcasted_iota(jnp.int32, (pack_rows, LANES), 1) < HEAD_DIM
    one_lo = jnp.where(lo_tile, 1.0, 0.0).astype(jnp.bfloat16)
    one_hi = jnp.where(lo_tile, 0.0, 1.0).astype(jnp.bfloat16)
    return pack_rows, one_lo, one_hi


def _split_heads(qp, lo):
    zero = jnp.zeros((), qp.dtype)
    return jnp.concatenate([jnp.where(lo, qp, zero), jnp.where(lo, zero, qp)], axis=0)


def _merge_heads(o, lo):
    return jnp.where(lo, o[0:PAIR], o[PAIR:2 * PAIR])


def _pair_start(pp):
    return pp * PAIR if isinstance(pp, int) else pl.multiple_of(pp * PAIR, PAIR)


def _q_rows(pp):
    return pl.ds(_pair_start(pp), PAIR)


def _band(pp, n_prev, width):
    if isinstance(pp, int):
        start = pp * PAIR - n_prev * CHUNK
        cut = max(0, -start)
        return pl.ds(start + cut, width - cut), cut
    assert (n_prev * CHUNK) % PAIR == 0
    return pl.ds(pl.multiple_of(pp * PAIR - n_prev * CHUNK, PAIR), width), 0


def _attn_a_kernel(zq_ref, zkv_ref, sink_ref, y_ref, bias_ref, s_ref, m_ref, p_ref, t_ref):
    n_pairs = zq_ref.shape[1] // PAIR

    @pl.when(pl.program_id(0) == 0)
    def _():
        i, jk, valid = _band_geometry(A_BAND, A_PREV)
        dist = jnp.abs(i - jk + A_PREV * CHUNK).astype(jnp.float32)
        for h in range(A_HEADS):
            slope = 2.0 ** (-8.0 * (h + 1) / A_HEADS) * LOG2E
            rows = slice((h % A_GROUP) * PAIR, (h % A_GROUP + 1) * PAIR)
            bias_ref[h // A_GROUP, rows, :] = jnp.where(valid, -slope * dist, NEG_BIG)

    lo = _head_masks()
    pack_rows, one_lo, one_hi = _head_ones_tiles()

    def unit(hp):
        g = hp // (A_GROUP // 2)
        bias_rows = slice((hp % (A_GROUP // 2)) * 2 * PAIR, (hp % (A_GROUP // 2) + 1) * 2 * PAIR)

        def scores(pp, slot):
            qp = zq_ref[0, _q_rows(pp), 128 * hp:128 * hp + 128]
            band, cut = _band(pp, A_PREV, A_BAND)
            k = zkv_ref[0, band, 128 * g:128 * g + 128]
            s = lax.dot_general(_split_heads(qp, lo), k, NT_DIMS,
                                preferred_element_type=jnp.float32)
            s = s + bias_ref[g, bias_rows, cut:]
            s_ref[hp, slot, :, cut:] = s
            m = jnp.max(s, axis=-1, keepdims=True)
            m_ref[hp, slot] = jnp.broadcast_to(m, (2 * PAIR, LANES))

        def weights(pp, slot):
            _, cut = _band(pp, A_PREV, A_BAND)
            m = m_ref[hp, slot]
            for t in range(cut // LANES, A_BAND // LANES):
                cols = slice(t * LANES, (t + 1) * LANES)
                e = jnp.exp2(s_ref[hp, slot, :, cols] - m).astype(jnp.bfloat16)
                p_ref[hp, slot, :, cols] = e[0:PAIR]
                p_ref[hp, slot, :, A_BAND + t * LANES:A_BAND + (t + 1) * LANES] = e[PAIR:2 * PAIR]
            t_ref[hp, slot] = jnp.exp2(sink_ref[hp] - _merge_heads(m, lo))

        def output(pp, slot):
            rows = _q_rows(pp)
            band, cut = _band(pp, A_PREV, A_BAND)
            v_first = zkv_ref[0, band, 256 + 256 * g:256 + 256 * g + 128]
            v_second = zkv_ref[0, band, 384 + 256 * g:384 + 256 * g + 128]
            e = jnp.concatenate([p_ref[hp, slot, :, cut:A_BAND], p_ref[hp, slot, :, A_BAND + cut:]],
                                axis=1)
            n_tiles = (A_BAND - cut) // pack_rows
            rhs = jnp.concatenate([
                jnp.concatenate([v_first, jnp.concatenate([one_lo] * n_tiles, axis=0)], axis=1),
                jnp.concatenate([v_second, jnp.concatenate([one_hi] * n_tiles, axis=0)], axis=1),
            ], axis=0)
            o = jnp.dot(e, rhs, preferred_element_type=jnp.float32)
            gate = zq_ref[0, rows, 512 + 128 * hp:512 + 128 * hp + 128].astype(jnp.float32)
            y_ref[0, rows, 128 * hp:128 * hp + 128] = (
                o[:, :LANES] / (o[:, LANES:] + t_ref[hp, slot]) * gate).astype(y_ref.dtype)

        return scores, weights, output

    _software_pipeline(n_pairs, [unit(hp) for hp in range(A_HEADS // 2)], A_PIPE_LAG,
                       -(-A_PREV * CHUNK // PAIR), min_fill=n_pairs)


def _attention_a(zq, zkv, sink):
    b, s, _ = zq.shape
    slots = PIPE_UNROLL
    rows = A_GROUP * PAIR
    units = A_HEADS // 2
    return pl.pallas_call(
        _attn_a_kernel,
        grid=(b,),
        in_specs=[
            pl.BlockSpec((1, s, ZQ_W), lambda bi: (bi, 0, 0)),
            pl.BlockSpec((1, s, ZKVA_W), lambda bi: (bi, 0, 0)),
            pl.BlockSpec(sink.shape, lambda bi: (0, 0, 0)),
        ],
        out_specs=pl.BlockSpec((1, s, A_W), lambda bi: (bi, 0, 0)),
        out_shape=jax.ShapeDtypeStruct((b, s, A_W), jnp.bfloat16),
        scratch_shapes=[
            pltpu.VMEM((A_KV_HEADS, rows, A_BAND), jnp.float32),
            pltpu.VMEM((units, slots, 2 * PAIR, A_BAND), jnp.float32),
            pltpu.VMEM((units, slots, 2 * PAIR, LANES), jnp.float32),
            pltpu.VMEM((units, slots, PAIR, 2 * A_BAND), jnp.bfloat16),
            pltpu.VMEM((units, slots, PAIR, LANES), jnp.float32),
        ],
        compiler_params=pltpu.CompilerParams(
            dimension_semantics=("arbitrary",),
            vmem_limit_bytes=VMEM_LIMIT),
        name="attention_a",
    )(zq, zkv, sink)


def _attn_b_kernel(zq_ref, zkv_ref, b0_ref, y_ref, bias_ref, s_ref, m_ref, p_ref):
    n_pairs = zq_ref.shape[1] // PAIR

    @pl.when(pl.program_id(0) == 0)
    def _():
        _, _, valid = _band_geometry(B_BAND, B_PREV)
        for h in range(B_HEADS):
            row = jnp.broadcast_to(b0_ref[h:h + 1, :], (PAIR, TOEP_W))
            toep = pltpu.roll(row, 0, axis=1, stride=1, stride_axis=0)[:, :B_BAND]
            rows = slice((h % 2) * PAIR, (h % 2 + 1) * PAIR)
            bias_ref[h // 2, rows, :] = jnp.where(valid, toep * LOG2E, NEG_BIG)

    lo = _head_masks()
    pack_rows, one_lo, one_hi = _head_ones_tiles()

    def unit(hp):
        def scores(pp, slot):
            qp = zq_ref[0, _q_rows(pp), 128 * hp:128 * hp + 128]
            band, cut = _band(pp, B_PREV, B_BAND)
            k = zkv_ref[0, band, 128 * hp:128 * hp + 128]
            s = lax.dot_general(_split_heads(qp, lo), k, NT_DIMS,
                                preferred_element_type=jnp.float32)
            s = s + bias_ref[hp, :, cut:]
            s_ref[hp, slot, :, cut:] = s
            m = jnp.max(s, axis=-1, keepdims=True)
            m_ref[hp, slot] = jnp.broadcast_to(m, (2 * PAIR, LANES))

        def weights(pp, slot):
            _, cut = _band(pp, B_PREV, B_BAND)
            m = m_ref[hp, slot]
            for t in range(cut // LANES, B_BAND // LANES):
                cols = slice(t * LANES, (t + 1) * LANES)
                e = jnp.exp2(s_ref[hp, slot, :, cols] - m).astype(jnp.bfloat16)
                p_ref[hp, slot, :, cols] = e[0:PAIR]
                p_ref[hp, slot, :, B_BAND + t * LANES:B_BAND + (t + 1) * LANES] = e[PAIR:2 * PAIR]

        def output(pp, slot):
            rows = _q_rows(pp)
            band, cut = _band(pp, B_PREV, B_BAND)
            v_h0 = zkv_ref[0, band, 512 + 128 * hp:512 + 128 * hp + 128]
            v_h1 = zkv_ref[0, band, 1024 + 128 * hp:1024 + 128 * hp + 128]
            e = jnp.concatenate([p_ref[hp, slot, :, cut:B_BAND], p_ref[hp, slot, :, B_BAND + cut:]],
                                axis=1)
            n_tiles = (B_BAND - cut) // pack_rows
            rhs = jnp.concatenate([
                jnp.concatenate([v_h0, jnp.concatenate([one_lo] * n_tiles, axis=0)], axis=1),
                jnp.concatenate([v_h1, jnp.concatenate([one_hi] * n_tiles, axis=0)], axis=1),
            ], axis=0)
            o = jnp.dot(e, rhs, preferred_element_type=jnp.float32)
            gate = zq_ref[0, rows, 512 + 128 * hp:512 + 128 * hp + 128].astype(jnp.float32)
            y_ref[0, rows, 128 * hp:128 * hp + 128] = (
                o[:, :LANES] / o[:, LANES:] * gate).astype(y_ref.dtype)

        return scores, weights, output

    _software_pipeline(n_pairs, [unit(hp) for hp in range(B_HEADS // 2)], B_PIPE_LAG,
                       -(-B_PREV * CHUNK // PAIR), min_fill=n_pairs)


def _toeplitz_base_rows(rel_bias_b):
    tbl = rel_bias_b.astype(jnp.float32)
    h = tbl.shape[0]
    far = tbl[:, 2 * MAX_REL:2 * MAX_REL + 1]
    n_far = B_PREV * CHUNK - MAX_REL
    n_mid = B_BAND - n_far
    assert n_mid <= 2 * MAX_REL
    mid = tbl[:, 2 * MAX_REL:2 * MAX_REL - n_mid:-1]
    return jnp.concatenate(
        [jnp.broadcast_to(far, (h, n_far)), mid, jnp.broadcast_to(far, (h, TOEP_W - B_BAND))], axis=1)


def _attention_b(zq, zkv, b0):
    b, s, _ = zq.shape
    slots = PIPE_UNROLL
    units = B_HEADS // 2
    return pl.pallas_call(
        _attn_b_kernel,
        grid=(b,),
        in_specs=[
            pl.BlockSpec((1, s, ZQ_W), lambda bi: (bi, 0, 0)),
            pl.BlockSpec((1, s, ZKVB_W), lambda bi: (bi, 0, 0)),
            pl.BlockSpec(b0.shape, lambda bi: (0, 0)),
        ],
        out_specs=pl.BlockSpec((1, s, B_W), lambda bi: (bi, 0, 0)),
        out_shape=jax.ShapeDtypeStruct((b, s, B_W), jnp.bfloat16),
        scratch_shapes=[
            pltpu.VMEM((units, 2 * PAIR, B_BAND), jnp.float32),
            pltpu.VMEM((units, slots, 2 * PAIR, B_BAND), jnp.float32),
            pltpu.VMEM((units, slots, 2 * PAIR, LANES), jnp.float32),
            pltpu.VMEM((units, slots, PAIR, 2 * B_BAND), jnp.bfloat16),
        ],
        compiler_params=pltpu.CompilerParams(
            dimension_semantics=("arbitrary",),
            vmem_limit_bytes=VMEM_LIMIT),
        name="attention_b",
    )(zq, zkv, b0)


def _out_proj_kernel(x_ref, ya_ref, yb_ref, p_ref, w_out_ref, ple_g_ref, w_gate_ref, w_proj_ref,
                     fin_g_ref, o_ref, u_ref):
    wa = ya_ref.shape[1]

    def residual(rows):
        h = (x_ref[rows, :]
             + jnp.dot(ya_ref[rows, :], w_out_ref[:wa, :], preferred_element_type=jnp.float32)
             + jnp.dot(yb_ref[rows, :], w_out_ref[wa:, :], preferred_element_type=jnp.float32))
        o_ref[rows, :] = h
        u_ref[rows, :] = (_rms_scale(h) * ple_g_ref[...]).astype(jnp.bfloat16)

    def embed(rows):
        gl = jnp.dot(u_ref[rows, :], w_gate_ref[...], preferred_element_type=jnp.float32)
        gate = 1.0 / (1.0 + jnp.exp(-gl))
        pe = jnp.dot(p_ref[rows, :].astype(jnp.bfloat16), w_proj_ref[...],
                     preferred_element_type=jnp.float32)
        h = o_ref[rows, :] + pe * gate
        o_ref[rows, :] = _rms_scale(h) * fin_g_ref[...]

    subs = [slice(r0, r0 + OUT_SUB_BLOCK) for r0 in range(0, TOK_BLOCK, OUT_SUB_BLOCK)]
    lead = 2
    for i in range(lead):
        residual(subs[i])
    for i in range(len(subs)):
        if i + lead < len(subs):
            residual(subs[i + lead])
        embed(subs[i])


def _out_proj(x2, ya2, yb2, p2, w_out, ple_g, w_gate, w_proj, fin_g):
    n, d = x2.shape
    pd = p2.shape[1]
    tok = lambda i: (i, 0)
    const = lambda i: (0, 0)
    return pl.pallas_call(
        _out_proj_kernel,
        grid=(n // TOK_BLOCK,),
        in_specs=[
            pl.BlockSpec((TOK_BLOCK, d), tok),
            pl.BlockSpec((TOK_BLOCK, ya2.shape[1]), tok),
            pl.BlockSpec((TOK_BLOCK, yb2.shape[1]), tok),
            pl.BlockSpec((TOK_BLOCK, pd), tok),
            pl.BlockSpec((d, d), const),
            pl.BlockSpec((1, d), const),
            pl.BlockSpec((d, d), const),
            pl.BlockSpec((pd, d), const),
            pl.BlockSpec((1, d), const),
        ],
        out_specs=pl.BlockSpec((TOK_BLOCK, d), tok),
        out_shape=jax.ShapeDtypeStruct((n, d), jnp.float32),
        scratch_shapes=[pltpu.VMEM((TOK_BLOCK, d), jnp.bfloat16)],
        compiler_params=pltpu.CompilerParams(
            dimension_semantics=("arbitrary",),
            vmem_limit_bytes=VMEM_LIMIT),
        name="out_proj",
    )(x2, ya2, yb2, p2, w_out, ple_g, w_gate, w_proj, fin_g)


@jax.jit
def kernel(x, p, norm_g, w_in, sink_a, rel_bias_b, w_out, ple_norm_g, w_ple_proj, w_ple_gate,
           final_norm_g):
    b, s, d = x.shape
    bf = jnp.bfloat16
    assert norm_g.shape[0] == 1, "the final RMSNorm is fused into the (single) layer's output kernel"
    zqa, zqb, zkva, zkvb = _in_proj(x, norm_g[0][None], w_in[0].astype(bf))
    sink = (sink_a[0].astype(jnp.float32) * LOG2E).reshape(A_HEADS // 2, 1, 2, 1)
    sink = jnp.broadcast_to(sink, (A_HEADS // 2, PAIR, 2, HEAD_DIM)).reshape(A_HEADS // 2, PAIR, LANES)
    ya = _attention_a(zqa, zkva, sink)
    yb = _attention_b(zqb, zkvb, _toeplitz_base_rows(rel_bias_b[0]))
    out = _out_proj(x.reshape(b * s, d), ya.reshape(b * s, -1), yb.reshape(b * s, -1),
                    p[0].reshape(b * s, -1), w_out[0].astype(bf), ple_norm_g[0][None],
                    w_ple_gate[0].astype(bf), w_ple_proj[0].astype(bf), final_norm_g[None])
    return out.reshape(b, s, d)
```

```python
import math

import jax
import jax.numpy as jnp
from jax import lax
from jax.experimental import pallas as pl
from jax.experimental.pallas import tpu as pltpu

D_MODEL = 1024
CHUNK = 64
HEAD_DIM = 64
A_HEADS = 8
A_KV_HEADS = 2
A_GROUP = A_HEADS // A_KV_HEADS
B_HEADS = 8
A_PREV = 2
B_PREV = 8
MAX_REL = 128
RMS_EPS = 1e-6
NEG_BIG = -1e30
LOG2E = math.log2(math.e)

A_Q = A_HEADS * HEAD_DIM
A_KV = A_KV_HEADS * HEAD_DIM
A_W = 512
B_W = 512
OFF_QA, OFF_KA, OFF_VA, OFF_GA = 0, 512, 640, 768
OFF_QB, OFF_KB, OFF_VB, OFF_GB = 1280, 1792, 2304, 2816
D_IN_PROJ = 3328

LANES = 128
PAIR = 2 * CHUNK
A_BAND = (A_PREV + 2) * CHUNK
B_BAND = (B_PREV + 2) * CHUNK
TOK_BLOCK = 1024
IN_SUB_BLOCK = 256
OUT_SUB_BLOCK = 256
TOEP_W = B_BAND + PAIR
ZQ_W = 1024
ZKVA_W = 768
ZKVB_W = 1536
PIPE_UNROLL = 4
A_PIPE_LAG = 1
B_PIPE_LAG = 2
VMEM_LIMIT = 56 * 1024 * 1024
Q_SCALE = HEAD_DIM ** -0.5 * LOG2E
NT_DIMS = (((1,), (1,)), ((), ()))


def _rms_scale(xf):
    var = jnp.mean(xf * xf, axis=-1, keepdims=True)
    return xf * lax.rsqrt(var + RMS_EPS)


def _silu(g):
    return g * (1.0 / (1.0 + jnp.exp(-g)))


def _dup_halves(v):
    rolled = pltpu.roll(v, HEAD_DIM, axis=1)
    lane = lax.broadcasted_iota(jnp.int32, v.shape, 1)
    lo = lane < HEAD_DIM
    return jnp.where(lo, v, rolled), jnp.where(lo, rolled, v)


def _cast_weight_once(is_first_step, w_f32_ref, w_ref):
    @pl.when(is_first_step)
    def _():
        for c0 in range(0, w_f32_ref.shape[1], 2 * LANES):
            cols = slice(c0, c0 + 2 * LANES)
            w_ref[:, cols] = w_f32_ref[:, cols].astype(w_ref.dtype)


def _in_proj_kernel(x_ref, g_ref, w_f32_ref, zqa_ref, zqb_ref, zkva_ref, zkvb_ref, w_ref):
    _cast_weight_once((pl.program_id(0) == 0) & (pl.program_id(1) == 0), w_f32_ref, w_ref)

    def project_rows(rows):
        u = (_rms_scale(x_ref[0, rows, :]) * g_ref[...]).astype(jnp.bfloat16)

        def proj(off, width):
            return jnp.dot(u, w_ref[:, off:off + width], preferred_element_type=jnp.float32)

        bf = jnp.bfloat16
        zqa_ref[0, rows, 0:512] = (proj(OFF_QA, A_Q) * Q_SCALE).astype(bf)
        zqa_ref[0, rows, 512:1024] = _silu(proj(OFF_GA, A_W)).astype(bf)
        zqb_ref[0, rows, 0:512] = (proj(OFF_QB, B_W) * Q_SCALE).astype(bf)
        zqb_ref[0, rows, 512:1024] = _silu(proj(OFF_GB, B_W)).astype(bf)
        assert OFF_VA == OFF_KA + A_KV
        kv = proj(OFF_KA, 2 * A_KV)
        kd0, kd1 = _dup_halves(kv[:, :A_KV])
        vd0, vd1 = _dup_halves(kv[:, A_KV:])
        first = lax.broadcasted_iota(jnp.int32, vd0.shape, 1) < HEAD_DIM
        zkva_ref[0, rows, 0:128] = kd0.astype(bf)
        zkva_ref[0, rows, 128:256] = kd1.astype(bf)
        for g, vd in enumerate((vd0, vd1)):
            zkva_ref[0, rows, 256 + 256 * g:384 + 256 * g] = jnp.where(first, vd, 0.0).astype(bf)
            zkva_ref[0, rows, 384 + 256 * g:512 + 256 * g] = jnp.where(first, 0.0, vd).astype(bf)
        zkvb_ref[0, rows, 0:512] = proj(OFF_KB, B_W).astype(bf)
        vb = proj(OFF_VB, B_W)
        even_head = lax.broadcasted_iota(jnp.int32, vb.shape, 1) % LANES < HEAD_DIM
        zkvb_ref[0, rows, 512:1024] = jnp.where(even_head, vb, 0.0).astype(bf)
        zkvb_ref[0, rows, 1024:1536] = jnp.where(even_head, 0.0, vb).astype(bf)

    for r0 in range(0, TOK_BLOCK, IN_SUB_BLOCK):
        project_rows(slice(r0, r0 + IN_SUB_BLOCK))


def _in_proj(x, norm_g, w_in):
    b, s, d = x.shape
    nblk = s // TOK_BLOCK
    assert nblk * TOK_BLOCK == s
    tok_idx = lambda bi, si: (bi, si, 0)
    bf = jnp.bfloat16
    return pl.pallas_call(
        _in_proj_kernel,
        grid=(b, nblk),
        in_specs=[
            pl.BlockSpec((1, TOK_BLOCK, d), tok_idx),
            pl.BlockSpec((1, d), lambda bi, si: (0, 0)),
            pl.BlockSpec((d, D_IN_PROJ), lambda bi, si: (0, 0), pipeline_mode=pl.Buffered(1)),
        ],
        out_specs=[
            pl.BlockSpec((1, TOK_BLOCK, ZQ_W), tok_idx),
            pl.BlockSpec((1, TOK_BLOCK, ZQ_W), tok_idx),
            pl.BlockSpec((1, TOK_BLOCK, ZKVA_W), tok_idx),
            pl.BlockSpec((1, TOK_BLOCK, ZKVB_W), tok_idx),
        ],
        out_shape=[
            jax.ShapeDtypeStruct((b, s, ZQ_W), bf),
            jax.ShapeDtypeStruct((b, s, ZQ_W), bf),
            jax.ShapeDtypeStruct((b, s, ZKVA_W), bf),
            jax.ShapeDtypeStruct((b, s, ZKVB_W), bf),
        ],
        scratch_shapes=[pltpu.VMEM((d, D_IN_PROJ), bf)],
        compiler_params=pltpu.CompilerParams(
            dimension_semantics=("arbitrary", "arbitrary"),
            vmem_limit_bytes=VMEM_LIMIT),
        name="in_proj",
    )(x, norm_g, w_in)


def _band_geometry(width, n_prev):
    i = lax.broadcasted_iota(jnp.int32, (PAIR, width), 0)
    jk = lax.broadcasted_iota(jnp.int32, (PAIR, width), 1)
    ci, kc = i // CHUNK, jk // CHUNK
    return i, jk, (kc >= ci) & (kc <= ci + n_prev)


def _pv_with_row_sums(e, v):
    v_ones = jnp.concatenate([v, jnp.ones(v.shape, v.dtype)], axis=1)
    o = jnp.dot(e, v_ones, preferred_element_type=jnp.float32)
    return o[:, :LANES], o[:, LANES:]


def _software_pipeline(n, units, lag, n_cut, min_fill=0):
    u = PIPE_UNROLL
    fill = max(-(-(n_cut + 2 * lag) // u) * u, min_fill)
    assert n % u == 0 and n >= fill and 2 * lag <= u

    def step(base, j, stage_range):
        for first, second, third in units:
            if 2 in stage_range:
                third(base + j - 2 * lag, (j - 2 * lag) % u)
            if 0 in stage_range:
                first(base + j, j % u)
            if 1 in stage_range:
                second(base + j - lag, (j - lag) % u)

    for j in range(fill):
        step(0, j, [k for k in range(3) if j >= k * lag])

    def steady(t, carry):
        for j in range(u):
            step(t * u, j, range(3))
        return carry

    if fill < n:
        lax.fori_loop(fill // u, n // u, steady, 0)
    for j in range(u, u + 2 * lag):
        step(n - u, j, [k for k in range(1, 3) if j - k * lag < u])


def _head_masks():
    lane = lax.broadcasted_iota(jnp.int32, (PAIR, LANES), 1)
    return lane < HEAD_DIM


def _head_ones_tiles():
    pack_rows = 16
    lo_tile = lax.broadcasted_iota(jnp.int32, (pack_rows, LANES), 1) < HEAD_DIM
    one_lo = jnp.where(lo_tile, 1.0, 0.0).astype(jnp.bfloat16)
    one_hi = jnp.where(lo_tile, 0.0, 1.0).astype(jnp.bfloat16)
    return pack_rows, one_lo, one_hi


def _split_heads(qp, lo):
    zero = jnp.zeros((), qp.dtype)
    return jnp.concatenate([jnp.where(lo, qp, zero), jnp.where(lo, zero, qp)], axis=0)


def _merge_heads(o, lo):
    return jnp.where(lo, o[0:PAIR], o[PAIR:2 * PAIR])


def _pair_start(pp):
    return pp * PAIR if isinstance(pp, int) else pl.multiple_of(pp * PAIR, PAIR)


def _q_rows(pp):
    return pl.ds(_pair_start(pp), PAIR)


def _band(pp, n_prev, width):
    if isinstance(pp, int):
        start = pp * PAIR - n_prev * CHUNK
        cut = max(0, -start)
        return pl.ds(start + cut, width - cut), cut
    assert (n_prev * CHUNK) % PAIR == 0
    return pl.ds(pl.multiple_of(pp * PAIR - n_prev * CHUNK, PAIR), width), 0


def _attn_a_kernel(zq_ref, zkv_ref, sink_ref, y_ref, bias_ref, s_ref, m_ref, p_ref, t_ref):
    n_pairs = zq_ref.shape[1] // PAIR

    @pl.when(pl.program_id(0) == 0)
    def _():
        i, jk, valid = _band_geometry(A_BAND, A_PREV)
        dist = jnp.abs(i - jk + A_PREV * CHUNK).astype(jnp.float32)
        for h in range(A_HEADS):
            slope = 2.0 ** (-8.0 * (h + 1) / A_HEADS) * LOG2E
            rows = slice((h % A_GROUP) * PAIR, (h % A_GROUP + 1) * PAIR)
            bias_ref[h // A_GROUP, rows, :] = jnp.where(valid, -slope * dist, NEG_BIG)

    lo = _head_masks()
    pack_rows, one_lo, one_hi = _head_ones_tiles()

    def unit(hp):
        g = hp // (A_GROUP // 2)
        bias_rows = slice((hp % (A_GROUP // 2)) * 2 * PAIR, (hp % (A_GROUP // 2) + 1) * 2 * PAIR)

        def scores(pp, slot):
            qp = zq_ref[0, _q_rows(pp), 128 * hp:128 * hp + 128]
            band, cut = _band(pp, A_PREV, A_BAND)
            k = zkv_ref[0, band, 128 * g:128 * g + 128]
            s = lax.dot_general(_split_heads(qp, lo), k, NT_DIMS,
                                preferred_element_type=jnp.float32)
            s = s + bias_ref[g, bias_rows, cut:]
            s_ref[hp, slot, :, cut:] = s
            m = jnp.max(s, axis=-1, keepdims=True)
            m_ref[hp, slot] = jnp.broadcast_to(m, (2 * PAIR, LANES))

        def weights(pp, slot):
            _, cut = _band(pp, A_PREV, A_BAND)
            m = m_ref[hp, slot]
            for t in range(cut // LANES, A_BAND // LANES):
                cols = slice(t * LANES, (t + 1) * LANES)
                e = jnp.exp2(s_ref[hp, slot, :, cols] - m).astype(jnp.bfloat16)
                p_ref[hp, slot, :, cols] = e[0:PAIR]
                p_ref[hp, slot, :, A_BAND + t * LANES:A_BAND + (t + 1) * LANES] = e[PAIR:2 * PAIR]
            t_ref[hp, slot] = jnp.exp2(sink_ref[hp] - _merge_heads(m, lo))

        def output(pp, slot):
            rows = _q_rows(pp)
            band, cut = _band(pp, A_PREV, A_BAND)
            v_first = zkv_ref[0, band, 256 + 256 * g:256 + 256 * g + 128]
            v_second = zkv_ref[0, band, 384 + 256 * g:384 + 256 * g + 128]
            e = jnp.concatenate([p_ref[hp, slot, :, cut:A_BAND], p_ref[hp, slot, :, A_BAND + cut:]],
                                axis=1)
            n_tiles = (A_BAND - cut) // pack_rows
            rhs = jnp.concatenate([
                jnp.concatenate([v_first, jnp.concatenate([one_lo] * n_tiles, axis=0)], axis=1),
                jnp.concatenate([v_second, jnp.concatenate([one_hi] * n_tiles, axis=0)], axis=1),
            ], axis=0)
            o = jnp.dot(e, rhs, preferred_element_type=jnp.float32)
            gate = zq_ref[0, rows, 512 + 128 * hp:512 + 128 * hp + 128].astype(jnp.float32)
            y_ref[0, rows, 128 * hp:128 * hp + 128] = (
                o[:, :LANES] / (o[:, LANES:] + t_ref[hp, slot]) * gate).astype(y_ref.dtype)

        return scores, weights, output

    _software_pipeline(n_pairs, [unit(hp) for hp in range(A_HEADS // 2)], A_PIPE_LAG,
                       -(-A_PREV * CHUNK // PAIR), min_fill=n_pairs)


def _attention_a(zq, zkv, sink):
    b, s, _ = zq.shape
    slots = PIPE_UNROLL
    rows = A_GROUP * PAIR
    units = A_HEADS // 2
    return pl.pallas_call(
        _attn_a_kernel,
        grid=(b,),
        in_specs=[
            pl.BlockSpec((1, s, ZQ_W), lambda bi: (bi, 0, 0)),
            pl.BlockSpec((1, s, ZKVA_W), lambda bi: (bi, 0, 0)),
            pl.BlockSpec(sink.shape, lambda bi: (0, 0, 0)),
        ],
        out_specs=pl.BlockSpec((1, s, A_W), lambda bi: (bi, 0, 0)),
        out_shape=jax.ShapeDtypeStruct((b, s, A_W), jnp.bfloat16),
        scratch_shapes=[
            pltpu.VMEM((A_KV_HEADS, rows, A_BAND), jnp.float32),
            pltpu.VMEM((units, slots, 2 * PAIR, A_BAND), jnp.float32),
            pltpu.VMEM((units, slots, 2 * PAIR, LANES), jnp.float32),
            pltpu.VMEM((units, slots, PAIR, 2 * A_BAND), jnp.bfloat16),
            pltpu.VMEM((units, slots, PAIR, LANES), jnp.float32),
        ],
        compiler_params=pltpu.CompilerParams(
            dimension_semantics=("arbitrary",),
            vmem_limit_bytes=VMEM_LIMIT),
        name="attention_a",
    )(zq, zkv, sink)


def _attn_b_kernel(zq_ref, zkv_ref, b0_ref, y_ref, bias_ref, s_ref, m_ref, p_ref):
    n_pairs = zq_ref.shape[1] // PAIR

    @pl.when(pl.program_id(0) == 0)
    def _():
        _, _, valid = _band_geometry(B_BAND, B_PREV)
        for h in range(B_HEADS):
            row = jnp.broadcast_to(b0_ref[h:h + 1, :], (PAIR, TOEP_W))
            toep = pltpu.roll(row, 0, axis=1, stride=1, stride_axis=0)[:, :B_BAND]
            rows = slice((h % 2) * PAIR, (h % 2 + 1) * PAIR)
            bias_ref[h // 2, rows, :] = jnp.where(valid, toep * LOG2E, NEG_BIG)

    lo = _head_masks()
    pack_rows, one_lo, one_hi = _head_ones_tiles()

    def unit(hp):
        def scores(pp, slot):
            qp = zq_ref[0, _q_rows(pp), 128 * hp:128 * hp + 128]
            band, cut = _band(pp, B_PREV, B_BAND)
            k = zkv_ref[0, band, 128 * hp:128 * hp + 128]
            s = lax.dot_general(_split_heads(qp, lo), k, NT_DIMS,
                                preferred_element_type=jnp.float32)
            s = s + bias_ref[hp, :, cut:]
            s_ref[hp, slot, :, cut:] = s
            m = jnp.max(s, axis=-1, keepdims=True)
            m_ref[hp, slot] = jnp.broadcast_to(m, (2 * PAIR, LANES))

        def weights(pp, slot):
            _, cut = _band(pp, B_PREV, B_BAND)
            m = m_ref[hp, slot]
            for t in range(cut // LANES, B_BAND // LANES):
                cols = slice(t * LANES, (t + 1) * LANES)
                e = jnp.exp2(s_ref[hp, slot, :, cols] - m).astype(jnp.bfloat16)
                p_ref[hp, slot, :, cols] = e[0:PAIR]
                p_ref[hp, slot, :, B_BAND + t * LANES:B_BAND + (t + 1) * LANES] = e[PAIR:2 * PAIR]

        def output(pp, slot):
            rows = _q_rows(pp)
            band, cut = _band(pp, B_PREV, B_BAND)
            v_h0 = zkv_ref[0, band, 512 + 128 * hp:512 + 128 * hp + 128]
            v_h1 = zkv_ref[0, band, 1024 + 128 * hp:1024 + 128 * hp + 128]
            e = jnp.concatenate([p_ref[hp, slot, :, cut:B_BAND], p_ref[hp, slot, :, B_BAND + cut:]],
                                axis=1)
            n_tiles = (B_BAND - cut) // pack_rows
            rhs = jnp.concatenate([
                jnp.concatenate([v_h0, jnp.concatenate([one_lo] * n_tiles, axis=0)], axis=1),
                jnp.concatenate([v_h1, jnp.concatenate([one_hi] * n_tiles, axis=0)], axis=1),
            ], axis=0)
            o = jnp.dot(e, rhs, preferred_element_type=jnp.float32)
            gate = zq_ref[0, rows, 512 + 128 * hp:512 + 128 * hp + 128].astype(jnp.float32)
            y_ref[0, rows, 128 * hp:128 * hp + 128] = (
                o[:, :LANES] / o[:, LANES:] * gate).astype(y_ref.dtype)

        return scores, weights, output

    _software_pipeline(n_pairs, [unit(hp) for hp in range(B_HEADS // 2)], B_PIPE_LAG,
                       -(-B_PREV * CHUNK // PAIR))


def _toeplitz_base_rows(rel_bias_b):
    tbl = rel_bias_b.astype(jnp.float32)
    h = tbl.shape[0]
    far = tbl[:, 2 * MAX_REL:2 * MAX_REL + 1]
    n_far = B_PREV * CHUNK - MAX_REL
    n_mid = B_BAND - n_far
    assert n_mid <= 2 * MAX_REL
    mid = tbl[:, 2 * MAX_REL:2 * MAX_REL - n_mid:-1]
    return jnp.concatenate(
        [jnp.broadcast_to(far, (h, n_far)), mid, jnp.broadcast_to(far, (h, TOEP_W - B_BAND))], axis=1)


def _attention_b(zq, zkv, b0):
    b, s, _ = zq.shape
    slots = PIPE_UNROLL
    units = B_HEADS // 2
    return pl.pallas_call(
        _attn_b_kernel,
        grid=(b,),
        in_specs=[
            pl.BlockSpec((1, s, ZQ_W), lambda bi: (bi, 0, 0)),
            pl.BlockSpec((1, s, ZKVB_W), lambda bi: (bi, 0, 0)),
            pl.BlockSpec(b0.shape, lambda bi: (0, 0)),
        ],
        out_specs=pl.BlockSpec((1, s, B_W), lambda bi: (bi, 0, 0)),
        out_shape=jax.ShapeDtypeStruct((b, s, B_W), jnp.bfloat16),
        scratch_shapes=[
            pltpu.VMEM((units, 2 * PAIR, B_BAND), jnp.float32),
            pltpu.VMEM((units, slots, 2 * PAIR, B_BAND), jnp.float32),
            pltpu.VMEM((units, slots, 2 * PAIR, LANES), jnp.float32),
            pltpu.VMEM((units, slots, PAIR, 2 * B_BAND), jnp.bfloat16),
        ],
        compiler_params=pltpu.CompilerParams(
            dimension_semantics=("arbitrary",),
            vmem_limit_bytes=VMEM_LIMIT),
        name="attention_b",
    )(zq, zkv, b0)


def _out_proj_kernel(x_ref, ya_ref, yb_ref, p_ref, w_out_f32_ref, ple_g_ref, w_gate_f32_ref,
                     w_proj_f32_ref, fin_g_ref, o_ref, u_ref, w_out_ref, w_gate_ref, w_proj_ref):
    wa = ya_ref.shape[1]
    first_step = pl.program_id(0) == 0
    _cast_weight_once(first_step, w_out_f32_ref, w_out_ref)
    _cast_weight_once(first_step, w_gate_f32_ref, w_gate_ref)
    _cast_weight_once(first_step, w_proj_f32_ref, w_proj_ref)

    def residual(rows):
        h = (x_ref[rows, :]
             + jnp.dot(ya_ref[rows, :], w_out_ref[:wa, :], preferred_element_type=jnp.float32)
             + jnp.dot(yb_ref[rows, :], w_out_ref[wa:, :], preferred_element_type=jnp.float32))
        o_ref[rows, :] = h
        u_ref[rows, :] = (_rms_scale(h) * ple_g_ref[...]).astype(jnp.bfloat16)

    def embed(rows):
        gl = jnp.dot(u_ref[rows, :], w_gate_ref[...], preferred_element_type=jnp.float32)
        gate = 1.0 / (1.0 + jnp.exp(-gl))
        pe = jnp.dot(p_ref[rows, :].astype(jnp.bfloat16), w_proj_ref[...],
                     preferred_element_type=jnp.float32)
        h = o_ref[rows, :] + pe * gate
        o_ref[rows, :] = _rms_scale(h) * fin_g_ref[...]

    subs = [slice(r0, r0 + OUT_SUB_BLOCK) for r0 in range(0, TOK_BLOCK, OUT_SUB_BLOCK)]
    lead = 2
    for i in range(lead):
        residual(subs[i])
    for i in range(len(subs)):
        if i + lead < len(subs):
            residual(subs[i + lead])
        embed(subs[i])


def _out_proj(x2, ya2, yb2, p2, w_out, ple_g, w_gate, w_proj, fin_g):
    n, d = x2.shape
    pd = p2.shape[1]
    tok = lambda i: (i, 0)
    const = lambda i: (0, 0)
    return pl.pallas_call(
        _out_proj_kernel,
        grid=(n // TOK_BLOCK,),
        in_specs=[
            pl.BlockSpec((TOK_BLOCK, d), tok),
            pl.BlockSpec((TOK_BLOCK, ya2.shape[1]), tok),
            pl.BlockSpec((TOK_BLOCK, yb2.shape[1]), tok),
            pl.BlockSpec((TOK_BLOCK, pd), tok),
            pl.BlockSpec((d, d), const, pipeline_mode=pl.Buffered(1)),
            pl.BlockSpec((1, d), const),
            pl.BlockSpec((d, d), const, pipeline_mode=pl.Buffered(1)),
            pl.BlockSpec((pd, d), const, pipeline_mode=pl.Buffered(1)),
            pl.BlockSpec((1, d), const),
        ],
        out_specs=pl.BlockSpec((TOK_BLOCK, d), tok),
        out_shape=jax.ShapeDtypeStruct((n, d), jnp.float32),
        scratch_shapes=[pltpu.VMEM((TOK_BLOCK, d), jnp.bfloat16),
                        pltpu.VMEM((d, d), jnp.bfloat16),
                        pltpu.VMEM((d, d), jnp.bfloat16),
                        pltpu.VMEM((pd, d), jnp.bfloat16)],
        compiler_params=pltpu.CompilerParams(
            dimension_semantics=("arbitrary",),
            vmem_limit_bytes=VMEM_LIMIT),
        name="out_proj",
    )(x2, ya2, yb2, p2, w_out, ple_g, w_gate, w_proj, fin_g)


@jax.jit
def kernel(x, p, norm_g, w_in, sink_a, rel_bias_b, w_out, ple_norm_g, w_ple_proj, w_ple_gate,
           final_norm_g):
    b, s, d = x.shape
    assert norm_g.shape[0] == 1, "the final RMSNorm is fused into the (single) layer's output kernel"
    zqa, zqb, zkva, zkvb = _in_proj(x, norm_g[0][None], w_in[0])
    sink = (sink_a[0].astype(jnp.float32) * LOG2E).reshape(A_HEADS // 2, 1, 2, 1)
    sink = jnp.broadcast_to(sink, (A_HEADS // 2, PAIR, 2, HEAD_DIM)).reshape(A_HEADS // 2, PAIR, LANES)
    ya = _attention_a(zqa, zkva, sink)
    yb = _attention_b(zqb, zkvb, _toeplitz_base_rows(rel_bias_b[0]))
    out = _out_proj(x.reshape(b * s, d), ya.reshape(b * s, -1), yb.reshape(b * s, -1),
                    p[0].reshape(b * s, -1), w_out[0], ple_norm_g[0][None],
                    w_ple_gate[0], w_ple_proj[0], final_norm_g[None])
    return out.reshape(b, s, d)
```

```python
import math

import jax
import jax.numpy as jnp
from jax import lax
from jax.experimental import pallas as pl
from jax.experimental.pallas import tpu as pltpu

D_MODEL = 1024
CHUNK = 64
HEAD_DIM = 64
A_HEADS = 8
A_KV_HEADS = 2
A_GROUP = A_HEADS // A_KV_HEADS
B_HEADS = 8
A_PREV = 2
B_PREV = 8
MAX_REL = 128
RMS_EPS = 1e-6
NEG_BIG = -1e30
LOG2E = math.log2(math.e)

A_Q = A_HEADS * HEAD_DIM
A_KV = A_KV_HEADS * HEAD_DIM
A_W = 512
B_W = 512
OFF_QA, OFF_KA, OFF_VA, OFF_GA = 0, 512, 640, 768
OFF_QB, OFF_KB, OFF_VB, OFF_GB = 1280, 1792, 2304, 2816
D_IN_PROJ = 3328

LANES = 128
PAIR = 2 * CHUNK
A_BAND = (A_PREV + 2) * CHUNK
B_BAND = (B_PREV + 2) * CHUNK
TOK_BLOCK = 1024
IN_SUB_BLOCK = 256
OUT_SUB_BLOCK = 256
TOEP_W = B_BAND + PAIR
ZQ_W = 1024
ZKVA_W = 768
ZKVB_W = 1536
PIPE_UNROLL = 4
A_PIPE_LAG = 1
B_PIPE_LAG = 2
VMEM_LIMIT = 56 * 1024 * 1024
Q_SCALE = HEAD_DIM ** -0.5 * LOG2E
NT_DIMS = (((1,), (1,)), ((), ()))


def _rms_scale(xf):
    var = jnp.mean(xf * xf, axis=-1, keepdims=True)
    return xf * lax.rsqrt(var + RMS_EPS)


def _silu(g):
    return g * (1.0 / (1.0 + jnp.exp(-g)))


def _dup_halves(v):
    rolled = pltpu.roll(v, HEAD_DIM, axis=1)
    lane = lax.broadcasted_iota(jnp.int32, v.shape, 1)
    lo = lane < HEAD_DIM
    return jnp.where(lo, v, rolled), jnp.where(lo, rolled, v)


def _cast_weight_once(is_first_step, w_f32_ref, w_ref):
    @pl.when(is_first_step)
    def _():
        for c0 in range(0, w_f32_ref.shape[1], 2 * LANES):
            cols = slice(c0, c0 + 2 * LANES)
            w_ref[:, cols] = w_f32_ref[:, cols].astype(w_ref.dtype)


def _in_proj_kernel(x_ref, g_ref, w_f32_ref, zqa_ref, zqb_ref, zkva_ref, zkvb_ref, w_ref):
    _cast_weight_once((pl.program_id(0) == 0) & (pl.program_id(1) == 0), w_f32_ref, w_ref)

    def project_rows(rows):
        u = (_rms_scale(x_ref[0, rows, :]) * g_ref[...]).astype(jnp.bfloat16)

        def proj(off, width):
            return jnp.dot(u, w_ref[:, off:off + width], preferred_element_type=jnp.float32)

        bf = jnp.bfloat16
        zqa_ref[0, rows, 0:512] = (proj(OFF_QA, A_Q) * Q_SCALE).astype(bf)
        zqa_ref[0, rows, 512:1024] = _silu(proj(OFF_GA, A_W)).astype(bf)
        zqb_ref[0, rows, 0:512] = (proj(OFF_QB, B_W) * Q_SCALE).astype(bf)
        zqb_ref[0, rows, 512:1024] = _silu(proj(OFF_GB, B_W)).astype(bf)
        assert OFF_VA == OFF_KA + A_KV
        kv = proj(OFF_KA, 2 * A_KV)
        kd0, kd1 = _dup_halves(kv[:, :A_KV])
        vd0, vd1 = _dup_halves(kv[:, A_KV:])
        first = lax.broadcasted_iota(jnp.int32, vd0.shape, 1) < HEAD_DIM
        zkva_ref[0, rows, 0:128] = kd0.astype(bf)
        zkva_ref[0, rows, 128:256] = kd1.astype(bf)
        for g, vd in enumerate((vd0, vd1)):
            zkva_ref[0, rows, 256 + 256 * g:384 + 256 * g] = jnp.where(first, vd, 0.0).astype(bf)
            zkva_ref[0, rows, 384 + 256 * g:512 + 256 * g] = jnp.where(first, 0.0, vd).astype(bf)
        zkvb_ref[0, rows, 0:512] = proj(OFF_KB, B_W).astype(bf)
        vb = proj(OFF_VB, B_W)
        even_head = lax.broadcasted_iota(jnp.int32, vb.shape, 1) % LANES < HEAD_DIM
        zkvb_ref[0, rows, 512:1024] = jnp.where(even_head, vb, 0.0).astype(bf)
        zkvb_ref[0, rows, 1024:1536] = jnp.where(even_head, 0.0, vb).astype(bf)

    for r0 in range(0, TOK_BLOCK, IN_SUB_BLOCK):
        project_rows(slice(r0, r0 + IN_SUB_BLOCK))


def _in_proj(x, norm_g, w_in):
    b, s, d = x.shape
    nblk = s // TOK_BLOCK
    assert nblk * TOK_BLOCK == s
    tok_idx = lambda bi, si: (bi, si, 0)
    bf = jnp.bfloat16
    return pl.pallas_call(
        _in_proj_kernel,
        grid=(b, nblk),
        in_specs=[
            pl.BlockSpec((1, TOK_BLOCK, d), tok_idx),
            pl.BlockSpec((1, d), lambda bi, si: (0, 0)),
            pl.BlockSpec((d, D_IN_PROJ), lambda bi, si: (0, 0), pipeline_mode=pl.Buffered(1)),
        ],
        out_specs=[
            pl.BlockSpec((1, TOK_BLOCK, ZQ_W), tok_idx),
            pl.BlockSpec((1, TOK_BLOCK, ZQ_W), tok_idx),
            pl.BlockSpec((1, TOK_BLOCK, ZKVA_W), tok_idx),
            pl.BlockSpec((1, TOK_BLOCK, ZKVB_W), tok_idx),
        ],
        out_shape=[
            jax.ShapeDtypeStruct((b, s, ZQ_W), bf),
            jax.ShapeDtypeStruct((b, s, ZQ_W), bf),
            jax.ShapeDtypeStruct((b, s, ZKVA_W), bf),
            jax.ShapeDtypeStruct((b, s, ZKVB_W), bf),
        ],
        scratch_shapes=[pltpu.VMEM((d, D_IN_PROJ), bf)],
        compiler_params=pltpu.CompilerParams(
            dimension_semantics=("arbitrary", "arbitrary"),
            vmem_limit_bytes=VMEM_LIMIT),
        name="in_proj",
    )(x, norm_g, w_in)


def _band_geometry(width, n_prev):
    i = lax.broadcasted_iota(jnp.int32, (PAIR, width), 0)
    jk = lax.broadcasted_iota(jnp.int32, (PAIR, width), 1)
    ci, kc = i // CHUNK, jk // CHUNK
    return i, jk, (kc >= ci) & (kc <= ci + n_prev)


def _pv_with_row_sums(e, v):
    v_ones = jnp.concatenate([v, jnp.ones(v.shape, v.dtype)], axis=1)
    o = jnp.dot(e, v_ones, preferred_element_type=jnp.float32)
    return o[:, :LANES], o[:, LANES:]


def _software_pipeline(n, units, lag, n_cut, min_fill=0):
    u = PIPE_UNROLL
    fill = max(-(-(n_cut + 2 * lag) // u) * u, min_fill)
    assert n % u == 0 and n >= fill and 2 * lag <= u

    def step(base, j, stage_range):
        for first, second, third in units:
            if 2 in stage_range:
                third(base + j - 2 * lag, (j - 2 * lag) % u)
            if 0 in stage_range:
                first(base + j, j % u)
            if 1 in stage_range:
                second(base + j - lag, (j - lag) % u)

    for j in range(fill):
        step(0, j, [k for k in range(3) if j >= k * lag])

    def steady(t, carry):
        for j in range(u):
            step(t * u, j, range(3))
        return carry

    if fill < n:
        lax.fori_loop(fill // u, n // u, steady, 0)
    for j in range(u, u + 2 * lag):
        step(n - u, j, [k for k in range(1, 3) if j - k * lag < u])


def _head_masks():
    lane = lax.broadcasted_iota(jnp.int32, (PAIR, LANES), 1)
    return lane < HEAD_DIM


def _head_ones_tiles():
    pack_rows = 16
    lo_tile = lax.broadcasted_iota(jnp.int32, (pack_rows, LANES), 1) < HEAD_DIM
    one_lo = jnp.where(lo_tile, 1.0, 0.0).astype(jnp.bfloat16)
    one_hi = jnp.where(lo_tile, 0.0, 1.0).astype(jnp.bfloat16)
    return pack_rows, one_lo, one_hi


def _split_heads(qp, lo):
    zero = jnp.zeros((), qp.dtype)
    return jnp.concatenate([jnp.where(lo, qp, zero), jnp.where(lo, zero, qp)], axis=0)


def _merge_heads(o, lo):
    return jnp.where(lo, o[0:PAIR], o[PAIR:2 * PAIR])


def _pair_start(pp):
    return pp * PAIR if isinstance(pp, int) else pl.multiple_of(pp * PAIR, PAIR)


def _q_rows(pp):
    return pl.ds(_pair_start(pp), PAIR)


def _band(pp, n_prev, width):
    if isinstance(pp, int):
        start = pp * PAIR - n_prev * CHUNK
        cut = max(0, -start)
        return pl.ds(start + cut, width - cut), cut
    assert (n_prev * CHUNK) % PAIR == 0
    return pl.ds(pl.multiple_of(pp * PAIR - n_prev * CHUNK, PAIR), width), 0


def _attn_a_kernel(zq_ref, zkv_ref, sink_ref, y_ref, bias_ref, s_ref, m_ref, p_ref, t_ref):
    n_pairs = zq_ref.shape[1] // PAIR

    @pl.when(pl.program_id(0) == 0)
    def _():
        i, jk, valid = _band_geometry(A_BAND, A_PREV)
        dist = jnp.abs(i - jk + A_PREV * CHUNK).astype(jnp.float32)
        for h in range(A_HEADS):
            slope = 2.0 ** (-8.0 * (h + 1) / A_HEADS) * LOG2E
            rows = slice((h % A_GROUP) * PAIR, (h % A_GROUP + 1) * PAIR)
            bias_ref[h // A_GROUP, rows, :] = jnp.where(valid, -slope * dist, NEG_BIG)

    lo = _head_masks()
    pack_rows, one_lo, one_hi = _head_ones_tiles()

    def unit(hp):
        g = hp // (A_GROUP // 2)
        bias_rows = slice((hp % (A_GROUP // 2)) * 2 * PAIR, (hp % (A_GROUP // 2) + 1) * 2 * PAIR)

        def scores(pp, slot):
            qp = zq_ref[0, _q_rows(pp), 128 * hp:128 * hp + 128]
            band, cut = _band(pp, A_PREV, A_BAND)
            k = zkv_ref[0, band, 128 * g:128 * g + 128]
            s = lax.dot_general(_split_heads(qp, lo), k, NT_DIMS,
                                preferred_element_type=jnp.float32)
            s = s + bias_ref[g, bias_rows, cut:]
            s_ref[hp, slot, :, cut:] = s
            m = jnp.max(s, axis=-1, keepdims=True)
            m_ref[hp, slot] = jnp.broadcast_to(m, (2 * PAIR, LANES))

        def weights(pp, slot):
            _, cut = _band(pp, A_PREV, A_BAND)
            m = m_ref[hp, slot]
            for t in range(cut // LANES, A_BAND // LANES):
                cols = slice(t * LANES, (t + 1) * LANES)
                e = jnp.exp2(s_ref[hp, slot, :, cols] - m).astype(jnp.bfloat16)
                p_ref[hp, slot, :, cols] = e[0:PAIR]
                p_ref[hp, slot, :, A_BAND + t * LANES:A_BAND + (t + 1) * LANES] = e[PAIR:2 * PAIR]
            t_ref[hp, slot] = jnp.exp2(sink_ref[hp] - _merge_heads(m, lo))

        def output(pp, slot):
            rows = _q_rows(pp)
            band, cut = _band(pp, A_PREV, A_BAND)
            v_first = zkv_ref[0, band, 256 + 256 * g:256 + 256 * g + 128]
            v_second = zkv_ref[0, band, 384 + 256 * g:384 + 256 * g + 128]
            e = jnp.concatenate([p_ref[hp, slot, :, cut:A_BAND], p_ref[hp, slot, :, A_BAND + cut:]],
                                axis=1)
            n_tiles = (A_BAND - cut) // pack_rows
            rhs = jnp.concatenate([
                jnp.concatenate([v_first, jnp.concatenate([one_lo] * n_tiles, axis=0)], axis=1),
                jnp.concatenate([v_second, jnp.concatenate([one_hi] * n_tiles, axis=0)], axis=1),
            ], axis=0)
            o = jnp.dot(e, rhs, preferred_element_type=jnp.float32)
            gate = zq_ref[0, rows, 512 + 128 * hp:512 + 128 * hp + 128].astype(jnp.float32)
            y_ref[0, rows, 128 * hp:128 * hp + 128] = (
                o[:, :LANES] / (o[:, LANES:] + t_ref[hp, slot]) * gate).astype(y_ref.dtype)

        return scores, weights, output

    for hp in range(0, A_HEADS // 2, 2):
        _software_pipeline(n_pairs, [unit(hp), unit(hp + 1)], A_PIPE_LAG,
                           -(-A_PREV * CHUNK // PAIR), min_fill=n_pairs)


def _attention_a(zq, zkv, sink):
    b, s, _ = zq.shape
    slots = PIPE_UNROLL
    rows = A_GROUP * PAIR
    units = A_HEADS // 2
    return pl.pallas_call(
        _attn_a_kernel,
        grid=(b,),
        in_specs=[
            pl.BlockSpec((1, s, ZQ_W), lambda bi: (bi, 0, 0)),
            pl.BlockSpec((1, s, ZKVA_W), lambda bi: (bi, 0, 0)),
            pl.BlockSpec(sink.shape, lambda bi: (0, 0, 0)),
        ],
        out_specs=pl.BlockSpec((1, s, A_W), lambda bi: (bi, 0, 0)),
        out_shape=jax.ShapeDtypeStruct((b, s, A_W), jnp.bfloat16),
        scratch_shapes=[
            pltpu.VMEM((A_KV_HEADS, rows, A_BAND), jnp.float32),
            pltpu.VMEM((units, slots, 2 * PAIR, A_BAND), jnp.float32),
            pltpu.VMEM((units, slots, 2 * PAIR, LANES), jnp.float32),
            pltpu.VMEM((units, slots, PAIR, 2 * A_BAND), jnp.bfloat16),
            pltpu.VMEM((units, slots, PAIR, LANES), jnp.float32),
        ],
        compiler_params=pltpu.CompilerParams(
            dimension_semantics=("arbitrary",),
            vmem_limit_bytes=VMEM_LIMIT),
        name="attention_a",
    )(zq, zkv, sink)


def _attn_b_kernel(zq_ref, zkv_ref, b0_ref, y_ref, bias_ref, s_ref, m_ref, p_ref):
    n_pairs = zq_ref.shape[1] // PAIR

    @pl.when(pl.program_id(0) == 0)
    def _():
        _, _, valid = _band_geometry(B_BAND, B_PREV)
        for h in range(B_HEADS):
            row = jnp.broadcast_to(b0_ref[h:h + 1, :], (PAIR, TOEP_W))
            toep = pltpu.roll(row, 0, axis=1, stride=1, stride_axis=0)[:, :B_BAND]
            rows = slice((h % 2) * PAIR, (h % 2 + 1) * PAIR)
            bias_ref[h // 2, rows, :] = jnp.where(valid, toep * LOG2E, NEG_BIG)

    lo = _head_masks()
    pack_rows, one_lo, one_hi = _head_ones_tiles()

    def unit(hp):
        def scores(pp, slot):
            qp = zq_ref[0, _q_rows(pp), 128 * hp:128 * hp + 128]
            band, cut = _band(pp, B_PREV, B_BAND)
            k = zkv_ref[0, band, 128 * hp:128 * hp + 128]
            s = lax.dot_general(_split_heads(qp, lo), k, NT_DIMS,
                                preferred_element_type=jnp.float32)
            s = s + bias_ref[hp, :, cut:]
            s_ref[hp, slot, :, cut:] = s
            m = jnp.max(s, axis=-1, keepdims=True)
            m_ref[hp, slot] = jnp.broadcast_to(m, (2 * PAIR, LANES))

        def weights(pp, slot):
            _, cut = _band(pp, B_PREV, B_BAND)
            m = m_ref[hp, slot]
            for t in range(cut // LANES, B_BAND // LANES):
                cols = slice(t * LANES, (t + 1) * LANES)
                e = jnp.exp2(s_ref[hp, slot, :, cols] - m).astype(jnp.bfloat16)
                p_ref[hp, slot, :, cols] = e[0:PAIR]
                p_ref[hp, slot, :, B_BAND + t * LANES:B_BAND + (t + 1) * LANES] = e[PAIR:2 * PAIR]

        def output(pp, slot):
            rows = _q_rows(pp)
            band, cut = _band(pp, B_PREV, B_BAND)
            v_h0 = zkv_ref[0, band, 512 + 128 * hp:512 + 128 * hp + 128]
            v_h1 = zkv_ref[0, band, 1024 + 128 * hp:1024 + 128 * hp + 128]
            e = jnp.concatenate([p_ref[hp, slot, :, cut:B_BAND], p_ref[hp, slot, :, B_BAND + cut:]],
                                axis=1)
            n_tiles = (B_BAND - cut) // pack_rows
            rhs = jnp.concatenate([
                jnp.concatenate([v_h0, jnp.concatenate([one_lo] * n_tiles, axis=0)], axis=1),
                jnp.concatenate([v_h1, jnp.concatenate([one_hi] * n_tiles, axis=0)], axis=1),
            ], axis=0)
            o = jnp.dot(e, rhs, preferred_element_type=jnp.float32)
            gate = zq_ref[0, rows, 512 + 128 * hp:512 + 128 * hp + 128].astype(jnp.float32)
            y_ref[0, rows, 128 * hp:128 * hp + 128] = (
                o[:, :LANES] / o[:, LANES:] * gate).astype(y_ref.dtype)

        return scores, weights, output

    _software_pipeline(n_pairs, [unit(hp) for hp in range(B_HEADS // 2)], B_PIPE_LAG,
                       -(-B_PREV * CHUNK // PAIR))


def _toeplitz_base_rows(rel_bias_b):
    tbl = rel_bias_b.astype(jnp.float32)
    h = tbl.shape[0]
    far = tbl[:, 2 * MAX_REL:2 * MAX_REL + 1]
    n_far = B_PREV * CHUNK - MAX_REL
    n_mid = B_BAND - n_far
    assert n_mid <= 2 * MAX_REL
    mid = tbl[:, 2 * MAX_REL:2 * MAX_REL - n_mid:-1]
    return jnp.concatenate(
        [jnp.broadcast_to(far, (h, n_far)), mid, jnp.broadcast_to(far, (h, TOEP_W - B_BAND))], axis=1)


def _attention_b(zq, zkv, b0):
    b, s, _ = zq.shape
    slots = PIPE_UNROLL
    units = B_HEADS // 2
    return pl.pallas_call(
        _attn_b_kernel,
        grid=(b,),
        in_specs=[
            pl.BlockSpec((1, s, ZQ_W), lambda bi: (bi, 0, 0)),
            pl.BlockSpec((1, s, ZKVB_W), lambda bi: (bi, 0, 0)),
            pl.BlockSpec(b0.shape, lambda bi: (0, 0)),
        ],
        out_specs=pl.BlockSpec((1, s, B_W), lambda bi: (bi, 0, 0)),
        out_shape=jax.ShapeDtypeStruct((b, s, B_W), jnp.bfloat16),
        scratch_shapes=[
            pltpu.VMEM((units, 2 * PAIR, B_BAND), jnp.float32),
            pltpu.VMEM((units, slots, 2 * PAIR, B_BAND), jnp.float32),
            pltpu.VMEM((units, slots, 2 * PAIR, LANES), jnp.float32),
            pltpu.VMEM((units, slots, PAIR, 2 * B_BAND), jnp.bfloat16),
        ],
        compiler_params=pltpu.CompilerParams(
            dimension_semantics=("arbitrary",),
            vmem_limit_bytes=VMEM_LIMIT),
        name="attention_b",
    )(zq, zkv, b0)


def _out_proj_kernel(x_ref, ya_ref, yb_ref, p_ref, w_out_f32_ref, ple_g_ref, w_gate_f32_ref,
                     w_proj_f32_ref, fin_g_ref, o_ref, u_ref, w_out_ref, w_gate_ref, w_proj_ref):
    wa = ya_ref.shape[1]
    first_step = pl.program_id(0) == 0
    _cast_weight_once(first_step, w_out_f32_ref, w_out_ref)
    _cast_weight_once(first_step, w_gate_f32_ref, w_gate_ref)
    _cast_weight_once(first_step, w_proj_f32_ref, w_proj_ref)

    def residual(rows):
        h = (x_ref[rows, :]
             + jnp.dot(ya_ref[rows, :], w_out_ref[:wa, :], preferred_element_type=jnp.float32)
             + jnp.dot(yb_ref[rows, :], w_out_ref[wa:, :], preferred_element_type=jnp.float32))
        o_ref[rows, :] = h
        u_ref[rows, :] = (_rms_scale(h) * ple_g_ref[...]).astype(jnp.bfloat16)

    def embed(rows):
        gl = jnp.dot(u_ref[rows, :], w_gate_ref[...], preferred_element_type=jnp.float32)
        gate = 1.0 / (1.0 + jnp.exp(-gl))
        pe = jnp.dot(p_ref[rows, :].astype(jnp.bfloat16), w_proj_ref[...],
                     preferred_element_type=jnp.float32)
        h = o_ref[rows, :] + pe * gate
        o_ref[rows, :] = _rms_scale(h) * fin_g_ref[...]

    subs = [slice(r0, r0 + OUT_SUB_BLOCK) for r0 in range(0, TOK_BLOCK, OUT_SUB_BLOCK)]
    for rows in subs:
        residual(rows)
    for rows in subs:
        embed(rows)


def _out_proj(x2, ya2, yb2, p2, w_out, ple_g, w_gate, w_proj, fin_g):
    n, d = x2.shape
    pd = p2.shape[1]
    tok = lambda i: (i, 0)
    const = lambda i: (0, 0)
    return pl.pallas_call(
        _out_proj_kernel,
        grid=(n // TOK_BLOCK,),
        in_specs=[
            pl.BlockSpec((TOK_BLOCK, d), tok),
            pl.BlockSpec((TOK_BLOCK, ya2.shape[1]), tok),
            pl.BlockSpec((TOK_BLOCK, yb2.shape[1]), tok),
            pl.BlockSpec((TOK_BLOCK, pd), tok),
            pl.BlockSpec((d, d), const, pipeline_mode=pl.Buffered(1)),
            pl.BlockSpec((1, d), const),
            pl.BlockSpec((d, d), const, pipeline_mode=pl.Buffered(1)),
            pl.BlockSpec((pd, d), const, pipeline_mode=pl.Buffered(1)),
            pl.BlockSpec((1, d), const),
        ],
        out_specs=pl.BlockSpec((TOK_BLOCK, d), tok),
        out_shape=jax.ShapeDtypeStruct((n, d), jnp.float32),
        scratch_shapes=[pltpu.VMEM((TOK_BLOCK, d), jnp.bfloat16),
                        pltpu.VMEM((d, d), jnp.bfloat16),
                        pltpu.VMEM((d, d), jnp.bfloat16),
                        pltpu.VMEM((pd, d), jnp.bfloat16)],
        compiler_params=pltpu.CompilerParams(
            dimension_semantics=("arbitrary",),
            vmem_limit_bytes=VMEM_LIMIT),
        name="out_proj",
    )(x2, ya2, yb2, p2, w_out, ple_g, w_gate, w_proj, fin_g)


@jax.jit
def kernel(x, p, norm_g, w_in, sink_a, rel_bias_b, w_out, ple_norm_g, w_ple_proj, w_ple_gate,
           final_norm_g):
    b, s, d = x.shape
    assert norm_g.shape[0] == 1, "the final RMSNorm is fused into the (single) layer's output kernel"
    zqa, zqb, zkva, zkvb = _in_proj(x, norm_g[0][None], w_in[0])
    sink = (sink_a[0].astype(jnp.float32) * LOG2E).reshape(A_HEADS // 2, 1, 2, 1)
    sink = jnp.broadcast_to(sink, (A_HEADS // 2, PAIR, 2, HEAD_DIM)).reshape(A_HEADS // 2, PAIR, LANES)
    ya = _attention_a(zqa, zkva, sink)
    yb = _attention_b(zqb, zkvb, _toeplitz_base_rows(rel_bias_b[0]))
    out = _out_proj(x.reshape(b * s, d), ya.reshape(b * s, -1), yb.reshape(b * s, -1),
                    p[0].reshape(b * s, -1), w_out[0], ple_norm_g[0][None],
                    w_ple_gate[0], w_ple_proj[0], final_norm_g[None])
    return out.reshape(b, s, d)
```

```python
import math

import jax
import jax.numpy as jnp
from jax import lax
from jax.experimental import pallas as pl
from jax.experimental.pallas import tpu as pltpu

D_MODEL = 1024
CHUNK = 64
HEAD_DIM = 64
A_HEADS = 8
A_KV_HEADS = 2
A_GROUP = A_HEADS // A_KV_HEADS
B_HEADS = 8
A_PREV = 2
B_PREV = 8
MAX_REL = 128
RMS_EPS = 1e-6
NEG_BIG = -1e30
LOG2E = math.log2(math.e)

A_Q = A_HEADS * HEAD_DIM
A_KV = A_KV_HEADS * HEAD_DIM
A_W = 512
B_W = 512
OFF_QA, OFF_KA, OFF_VA, OFF_GA = 0, 512, 640, 768
OFF_QB, OFF_KB, OFF_VB, OFF_GB = 1280, 1792, 2304, 2816
D_IN_PROJ = 3328

LANES = 128
PAIR = 2 * CHUNK
A_BAND = (A_PREV + 2) * CHUNK
B_BAND = (B_PREV + 2) * CHUNK
TOK_BLOCK = 1024
IN_SUB_BLOCK = 256
OUT_SUB_BLOCK = 256
TOEP_W = B_BAND + PAIR
ZQ_W = 1024
ZKVA_W = 768
ZKVB_W = 1536
PIPE_UNROLL = 4
A_PIPE_LAG = 1
B_PIPE_LAG = 2
VMEM_LIMIT = 56 * 1024 * 1024
Q_SCALE = HEAD_DIM ** -0.5 * LOG2E
NT_DIMS = (((1,), (1,)), ((), ()))


def _rms_scale(xf):
    var = jnp.mean(xf * xf, axis=-1, keepdims=True)
    return xf * lax.rsqrt(var + RMS_EPS)


def _silu(g):
    return g * (1.0 / (1.0 + jnp.exp(-g)))


def _dup_halves(v):
    rolled = pltpu.roll(v, HEAD_DIM, axis=1)
    lane = lax.broadcasted_iota(jnp.int32, v.shape, 1)
    lo = lane < HEAD_DIM
    return jnp.where(lo, v, rolled), jnp.where(lo, rolled, v)


def _cast_weight_once(is_first_step, w_f32_ref, w_ref):
    @pl.when(is_first_step)
    def _():
        for c0 in range(0, w_f32_ref.shape[1], 2 * LANES):
            cols = slice(c0, c0 + 2 * LANES)
            w_ref[:, cols] = w_f32_ref[:, cols].astype(w_ref.dtype)


def _in_proj_kernel(x_ref, g_ref, w_f32_ref, zqa_ref, zqb_ref, zkva_ref, zkvb_ref, w_ref):
    _cast_weight_once((pl.program_id(0) == 0) & (pl.program_id(1) == 0), w_f32_ref, w_ref)

    def project_rows(rows):
        u = (_rms_scale(x_ref[0, rows, :]) * g_ref[...]).astype(jnp.bfloat16)

        def proj(off, width):
            return jnp.dot(u, w_ref[:, off:off + width], preferred_element_type=jnp.float32)

        bf = jnp.bfloat16
        zqa_ref[0, rows, 0:512] = (proj(OFF_QA, A_Q) * Q_SCALE).astype(bf)
        zqa_ref[0, rows, 512:1024] = _silu(proj(OFF_GA, A_W)).astype(bf)
        zqb_ref[0, rows, 0:512] = (proj(OFF_QB, B_W) * Q_SCALE).astype(bf)
        zqb_ref[0, rows, 512:1024] = _silu(proj(OFF_GB, B_W)).astype(bf)
        assert OFF_VA == OFF_KA + A_KV
        kv = proj(OFF_KA, 2 * A_KV)
        kd0, kd1 = _dup_halves(kv[:, :A_KV])
        vd0, vd1 = _dup_halves(kv[:, A_KV:])
        first = lax.broadcasted_iota(jnp.int32, vd0.shape, 1) < HEAD_DIM
        zkva_ref[0, rows, 0:128] = kd0.astype(bf)
        zkva_ref[0, rows, 128:256] = kd1.astype(bf)
        for g, vd in enumerate((vd0, vd1)):
            zkva_ref[0, rows, 256 + 256 * g:384 + 256 * g] = jnp.where(first, vd, 0.0).astype(bf)
            zkva_ref[0, rows, 384 + 256 * g:512 + 256 * g] = jnp.where(first, 0.0, vd).astype(bf)
        zkvb_ref[0, rows, 0:512] = proj(OFF_KB, B_W).astype(bf)
        vb = proj(OFF_VB, B_W)
        even_head = lax.broadcasted_iota(jnp.int32, vb.shape, 1) % LANES < HEAD_DIM
        zkvb_ref[0, rows, 512:1024] = jnp.where(even_head, vb, 0.0).astype(bf)
        zkvb_ref[0, rows, 1024:1536] = jnp.where(even_head, 0.0, vb).astype(bf)

    for r0 in range(0, TOK_BLOCK, IN_SUB_BLOCK):
        project_rows(slice(r0, r0 + IN_SUB_BLOCK))


def _in_proj(x, norm_g, w_in):
    b, s, d = x.shape
    nblk = s // TOK_BLOCK
    assert nblk * TOK_BLOCK == s
    tok_idx = lambda bi, si: (bi, si, 0)
    bf = jnp.bfloat16
    return pl.pallas_call(
        _in_proj_kernel,
        grid=(b, nblk),
        in_specs=[
            pl.BlockSpec((1, TOK_BLOCK, d), tok_idx),
            pl.BlockSpec((1, d), lambda bi, si: (0, 0)),
            pl.BlockSpec((d, D_IN_PROJ), lambda bi, si: (0, 0), pipeline_mode=pl.Buffered(1)),
        ],
        out_specs=[
            pl.BlockSpec((1, TOK_BLOCK, ZQ_W), tok_idx),
            pl.BlockSpec((1, TOK_BLOCK, ZQ_W), tok_idx),
            pl.BlockSpec((1, TOK_BLOCK, ZKVA_W), tok_idx),
            pl.BlockSpec((1, TOK_BLOCK, ZKVB_W), tok_idx),
        ],
        out_shape=[
            jax.ShapeDtypeStruct((b, s, ZQ_W), bf),
            jax.ShapeDtypeStruct((b, s, ZQ_W), bf),
            jax.ShapeDtypeStruct((b, s, ZKVA_W), bf),
            jax.ShapeDtypeStruct((b, s, ZKVB_W), bf),
        ],
        scratch_shapes=[pltpu.VMEM((d, D_IN_PROJ), bf)],
        compiler_params=pltpu.CompilerParams(
            dimension_semantics=("arbitrary", "arbitrary"),
            vmem_limit_bytes=VMEM_LIMIT),
        name="in_proj",
    )(x, norm_g, w_in)


def _band_geometry(width, n_prev):
    i = lax.broadcasted_iota(jnp.int32, (PAIR, width), 0)
    jk = lax.broadcasted_iota(jnp.int32, (PAIR, width), 1)
    ci, kc = i // CHUNK, jk // CHUNK
    return i, jk, (kc >= ci) & (kc <= ci + n_prev)


def _pv_with_row_sums(e, v):
    v_ones = jnp.concatenate([v, jnp.ones(v.shape, v.dtype)], axis=1)
    o = jnp.dot(e, v_ones, preferred_element_type=jnp.float32)
    return o[:, :LANES], o[:, LANES:]


def _software_pipeline(n, units, lag, n_cut, min_fill=0):
    u = PIPE_UNROLL
    fill = max(-(-(n_cut + 2 * lag) // u) * u, min_fill)
    assert n % u == 0 and n >= fill and 2 * lag <= u

    def step(base, j, stage_range):
        for first, second, third in units:
            if 2 in stage_range:
                third(base + j - 2 * lag, (j - 2 * lag) % u)
            if 0 in stage_range:
                first(base + j, j % u)
            if 1 in stage_range:
                second(base + j - lag, (j - lag) % u)

    for j in range(fill):
        step(0, j, [k for k in range(3) if j >= k * lag])

    def steady(t, carry):
        for j in range(u):
            step(t * u, j, range(3))
        return carry

    if fill < n:
        lax.fori_loop(fill // u, n // u, steady, 0)
    for j in range(u, u + 2 * lag):
        step(n - u, j, [k for k in range(1, 3) if j - k * lag < u])


def _head_masks():
    lane = lax.broadcasted_iota(jnp.int32, (PAIR, LANES), 1)
    return lane < HEAD_DIM


def _head_ones_tiles():
    pack_rows = 16
    lo_tile = lax.broadcasted_iota(jnp.int32, (pack_rows, LANES), 1) < HEAD_DIM
    one_lo = jnp.where(lo_tile, 1.0, 0.0).astype(jnp.bfloat16)
    one_hi = jnp.where(lo_tile, 0.0, 1.0).astype(jnp.bfloat16)
    return pack_rows, one_lo, one_hi


def _split_heads(qp, lo):
    zero = jnp.zeros((), qp.dtype)
    return jnp.concatenate([jnp.where(lo, qp, zero), jnp.where(lo, zero, qp)], axis=0)


def _merge_heads(o, lo):
    return jnp.where(lo, o[0:PAIR], o[PAIR:2 * PAIR])


def _pair_start(pp):
    return pp * PAIR if isinstance(pp, int) else pl.multiple_of(pp * PAIR, PAIR)


def _q_rows(pp):
    return pl.ds(_pair_start(pp), PAIR)


def _band(pp, n_prev, width):
    if isinstance(pp, int):
        start = pp * PAIR - n_prev * CHUNK
        cut = max(0, -start)
        return pl.ds(start + cut, width - cut), cut
    assert (n_prev * CHUNK) % PAIR == 0
    return pl.ds(pl.multiple_of(pp * PAIR - n_prev * CHUNK, PAIR), width), 0


def _attn_a_kernel(zq_ref, zkv_ref, sink_ref, y_ref, bias_ref, s_ref, m_ref, p_ref, t_ref):
    n_pairs = zq_ref.shape[1] // PAIR

    @pl.when(pl.program_id(0) == 0)
    def _():
        i, jk, valid = _band_geometry(A_BAND, A_PREV)
        dist = jnp.abs(i - jk + A_PREV * CHUNK).astype(jnp.float32)
        for h in range(A_HEADS):
            slope = 2.0 ** (-8.0 * (h + 1) / A_HEADS) * LOG2E
            rows = slice((h % A_GROUP) * PAIR, (h % A_GROUP + 1) * PAIR)
            bias_ref[h // A_GROUP, rows, :] = jnp.where(valid, -slope * dist, NEG_BIG)

    lo = _head_masks()
    pack_rows, one_lo, one_hi = _head_ones_tiles()

    def unit(hp):
        g = hp // (A_GROUP // 2)
        bias_rows = slice((hp % (A_GROUP // 2)) * 2 * PAIR, (hp % (A_GROUP // 2) + 1) * 2 * PAIR)

        def scores(pp, slot):
            qp = zq_ref[0, _q_rows(pp), 128 * hp:128 * hp + 128]
            band, cut = _band(pp, A_PREV, A_BAND)
            k = zkv_ref[0, band, 128 * g:128 * g + 128]
            s = lax.dot_general(_split_heads(qp, lo), k, NT_DIMS,
                                preferred_element_type=jnp.float32)
            s = s + bias_ref[g, bias_rows, cut:]
            s_ref[hp, slot, :, cut:] = s
            m = jnp.max(s, axis=-1, keepdims=True)
            m_ref[hp, slot] = jnp.broadcast_to(m, (2 * PAIR, LANES))

        def weights(pp, slot):
            _, cut = _band(pp, A_PREV, A_BAND)
            m = m_ref[hp, slot]
            for t in range(cut // LANES, A_BAND // LANES):
                cols = slice(t * LANES, (t + 1) * LANES)
                e = jnp.exp2(s_ref[hp, slot, :, cols] - m).astype(jnp.bfloat16)
                p_ref[hp, slot, :, cols] = e[0:PAIR]
                p_ref[hp, slot, :, A_BAND + t * LANES:A_BAND + (t + 1) * LANES] = e[PAIR:2 * PAIR]
            t_ref[hp, slot] = jnp.exp2(sink_ref[hp] - _merge_heads(m, lo))

        def output(pp, slot):
            rows = _q_rows(pp)
            band, cut = _band(pp, A_PREV, A_BAND)
            v_first = zkv_ref[0, band, 256 + 256 * g:256 + 256 * g + 128]
            v_second = zkv_ref[0, band, 384 + 256 * g:384 + 256 * g + 128]
            e = jnp.concatenate([p_ref[hp, slot, :, cut:A_BAND], p_ref[hp, slot, :, A_BAND + cut:]],
                                axis=1)
            n_tiles = (A_BAND - cut) // pack_rows
            rhs = jnp.concatenate([
                jnp.concatenate([v_first, jnp.concatenate([one_lo] * n_tiles, axis=0)], axis=1),
                jnp.concatenate([v_second, jnp.concatenate([one_hi] * n_tiles, axis=0)], axis=1),
            ], axis=0)
            o = jnp.dot(e, rhs, preferred_element_type=jnp.float32)
            gate = zq_ref[0, rows, 512 + 128 * hp:512 + 128 * hp + 128].astype(jnp.float32)
            y_ref[0, rows, 128 * hp:128 * hp + 128] = (
                o[:, :LANES] / (o[:, LANES:] + t_ref[hp, slot]) * gate).astype(y_ref.dtype)

        return scores, weights, output

    _software_pipeline(n_pairs, [unit(hp) for hp in range(A_HEADS // 2)], A_PIPE_LAG,
                       -(-A_PREV * CHUNK // PAIR), min_fill=n_pairs)


def _attention_a(zq, zkv, sink):
    b, s, _ = zq.shape
    slots = PIPE_UNROLL
    rows = A_GROUP * PAIR
    units = A_HEADS // 2
    return pl.pallas_call(
        _attn_a_kernel,
        grid=(b,),
        in_specs=[
            pl.BlockSpec((1, s, ZQ_W), lambda bi: (bi, 0, 0)),
            pl.BlockSpec((1, s, ZKVA_W), lambda bi: (bi, 0, 0)),
            pl.BlockSpec(sink.shape, lambda bi: (0, 0, 0)),
        ],
        out_specs=pl.BlockSpec((1, s, A_W), lambda bi: (bi, 0, 0)),
        out_shape=jax.ShapeDtypeStruct((b, s, A_W), jnp.bfloat16),
        scratch_shapes=[
            pltpu.VMEM((A_KV_HEADS, rows, A_BAND), jnp.float32),
            pltpu.VMEM((units, slots, 2 * PAIR, A_BAND), jnp.float32),
            pltpu.VMEM((units, slots, 2 * PAIR, LANES), jnp.float32),
            pltpu.VMEM((units, slots, PAIR, 2 * A_BAND), jnp.bfloat16),
            pltpu.VMEM((units, slots, PAIR, LANES), jnp.float32),
        ],
        compiler_params=pltpu.CompilerParams(
            dimension_semantics=("arbitrary",),
            vmem_limit_bytes=VMEM_LIMIT),
        name="attention_a",
    )(zq, zkv, sink)


def _attn_b_kernel(zq_ref, zkv_ref, b0_ref, y_ref, bias_ref, s_ref, m_ref, p_ref):
    n_pairs = zq_ref.shape[1] // PAIR

    @pl.when(pl.program_id(0) == 0)
    def _():
        _, _, valid = _band_geometry(B_BAND, B_PREV)
        for h in range(B_HEADS):
            row = jnp.broadcast_to(b0_ref[h:h + 1, :], (PAIR, TOEP_W))
            toep = pltpu.roll(row, 0, axis=1, stride=1, stride_axis=0)[:, :B_BAND]
            rows = slice((h % 2) * PAIR, (h % 2 + 1) * PAIR)
            bias_ref[h // 2, rows, :] = jnp.where(valid, toep * LOG2E, NEG_BIG)

    lo = _head_masks()
    pack_rows, one_lo, one_hi = _head_ones_tiles()

    def unit(hp):
        def scores(pp, slot):
            qp = zq_ref[0, _q_rows(pp), 128 * hp:128 * hp + 128]
            band, cut = _band(pp, B_PREV, B_BAND)
            k = zkv_ref[0, band, 128 * hp:128 * hp + 128]
            s = lax.dot_general(_split_heads(qp, lo), k, NT_DIMS,
                                preferred_element_type=jnp.float32)
            s = s + bias_ref[hp, :, cut:]
            s_ref[hp, slot, :, cut:] = s
            m = jnp.max(s, axis=-1, keepdims=True)
            m_ref[hp, slot] = jnp.broadcast_to(m, (2 * PAIR, LANES))

        def weights(pp, slot):
            _, cut = _band(pp, B_PREV, B_BAND)
            m = m_ref[hp, slot]
            for t in range(cut // LANES, B_BAND // LANES):
                cols = slice(t * LANES, (t + 1) * LANES)
                e = jnp.exp2(s_ref[hp, slot, :, cols] - m).astype(jnp.bfloat16)
                p_ref[hp, slot, :, cols] = e[0:PAIR]
                p_ref[hp, slot, :, B_BAND + t * LANES:B_BAND + (t + 1) * LANES] = e[PAIR:2 * PAIR]

        def output(pp, slot):
            rows = _q_rows(pp)
            band, cut = _band(pp, B_PREV, B_BAND)
            v_h0 = zkv_ref[0, band, 512 + 128 * hp:512 + 128 * hp + 128]
            v_h1 = zkv_ref[0, band, 1024 + 128 * hp:1024 + 128 * hp + 128]
            e = jnp.concatenate([p_ref[hp, slot, :, cut:B_BAND], p_ref[hp, slot, :, B_BAND + cut:]],
                                axis=1)
            n_tiles = (B_BAND - cut) // pack_rows
            rhs = jnp.concatenate([
                jnp.concatenate([v_h0, jnp.concatenate([one_lo] * n_tiles, axis=0)], axis=1),
                jnp.concatenate([v_h1, jnp.concatenate([one_hi] * n_tiles, axis=0)], axis=1),
            ], axis=0)
            o = jnp.dot(e, rhs, preferred_element_type=jnp.float32)
            gate = zq_ref[0, rows, 512 + 128 * hp:512 + 128 * hp + 128].astype(jnp.float32)
            y_ref[0, rows, 128 * hp:128 * hp + 128] = (
                o[:, :LANES] / o[:, LANES:] * gate).astype(y_ref.dtype)

        return scores, weights, output

    _software_pipeline(n_pairs, [unit(hp) for hp in range(B_HEADS // 2)], B_PIPE_LAG,
                       -(-B_PREV * CHUNK // PAIR))


def _toeplitz_base_rows(rel_bias_b):
    tbl = rel_bias_b.astype(jnp.float32)
    h = tbl.shape[0]
    far = tbl[:, 2 * MAX_REL:2 * MAX_REL + 1]
    n_far = B_PREV * CHUNK - MAX_REL
    n_mid = B_BAND - n_far
    assert n_mid <= 2 * MAX_REL
    mid = tbl[:, 2 * MAX_REL:2 * MAX_REL - n_mid:-1]
    return jnp.concatenate(
        [jnp.broadcast_to(far, (h, n_far)), mid, jnp.broadcast_to(far, (h, TOEP_W - B_BAND))], axis=1)


def _attention_b(zq, zkv, b0):
    b, s, _ = zq.shape
    slots = PIPE_UNROLL
    units = B_HEADS // 2
    return pl.pallas_call(
        _attn_b_kernel,
        grid=(b,),
        in_specs=[
            pl.BlockSpec((1, s, ZQ_W), lambda bi: (bi, 0, 0)),
            pl.BlockSpec((1, s, ZKVB_W), lambda bi: (bi, 0, 0)),
            pl.BlockSpec(b0.shape, lambda bi: (0, 0)),
        ],
        out_specs=pl.BlockSpec((1, s, B_W), lambda bi: (bi, 0, 0)),
        out_shape=jax.ShapeDtypeStruct((b, s, B_W), jnp.bfloat16),
        scratch_shapes=[
            pltpu.VMEM((units, 2 * PAIR, B_BAND), jnp.float32),
            pltpu.VMEM((units, slots, 2 * PAIR, B_BAND), jnp.float32),
            pltpu.VMEM((units, slots, 2 * PAIR, LANES), jnp.float32),
            pltpu.VMEM((units, slots, PAIR, 2 * B_BAND), jnp.bfloat16),
        ],
        compiler_params=pltpu.CompilerParams(
            dimension_semantics=("arbitrary",),
            vmem_limit_bytes=VMEM_LIMIT),
        name="attention_b",
    )(zq, zkv, b0)


def _out_proj_kernel(x_ref, ya_ref, yb_ref, p_ref, w_out_f32_ref, ple_g_ref, w_gate_f32_ref,
                     w_proj_f32_ref, fin_g_ref, o_ref, u_ref, w_out_ref, w_gate_ref, w_proj_ref):
    wa = ya_ref.shape[1]
    first_step = pl.program_id(0) == 0
    _cast_weight_once(first_step, w_out_f32_ref, w_out_ref)
    _cast_weight_once(first_step, w_gate_f32_ref, w_gate_ref)
    _cast_weight_once(first_step, w_proj_f32_ref, w_proj_ref)

    def residual(rows):
        h = (x_ref[rows, :]
             + jnp.dot(ya_ref[rows, :], w_out_ref[:wa, :], preferred_element_type=jnp.float32)
             + jnp.dot(yb_ref[rows, :], w_out_ref[wa:, :], preferred_element_type=jnp.float32))
        o_ref[rows, :] = h
        u_ref[rows, :] = (_rms_scale(h) * ple_g_ref[...]).astype(jnp.bfloat16)

    def embed(rows):
        gl = jnp.dot(u_ref[rows, :], w_gate_ref[...], preferred_element_type=jnp.float32)
        gate = 1.0 / (1.0 + jnp.exp(-gl))
        pe = jnp.dot(p_ref[rows, :].astype(jnp.bfloat16), w_proj_ref[...],
                     preferred_element_type=jnp.float32)
        h = o_ref[rows, :] + pe * gate
        o_ref[rows, :] = _rms_scale(h) * fin_g_ref[...]

    subs = [slice(r0, r0 + OUT_SUB_BLOCK) for r0 in range(0, TOK_BLOCK, OUT_SUB_BLOCK)]
    for rows in subs:
        residual(rows)
    for rows in subs:
        embed(rows)


def _out_proj(x2, ya2, yb2, p2, w_out, ple_g, w_gate, w_proj, fin_g):
    n, d = x2.shape
    pd = p2.shape[1]
    tok = lambda i: (i, 0)
    const = lambda i: (0, 0)
    return pl.pallas_call(
        _out_proj_kernel,
        grid=(n // TOK_BLOCK,),
        in_specs=[
            pl.BlockSpec((TOK_BLOCK, d), tok),
            pl.BlockSpec((TOK_BLOCK, ya2.shape[1]), tok),
            pl.BlockSpec((TOK_BLOCK, yb2.shape[1]), tok),
            pl.BlockSpec((TOK_BLOCK, pd), tok),
            pl.BlockSpec((d, d), const, pipeline_mode=pl.Buffered(1)),
            pl.BlockSpec((1, d), const),
            pl.BlockSpec((d, d), const, pipeline_mode=pl.Buffered(1)),
            pl.BlockSpec((pd, d), const, pipeline_mode=pl.Buffered(1)),
            pl.BlockSpec((1, d), const),
        ],
        out_specs=pl.BlockSpec((TOK_BLOCK, d), tok),
        out_shape=jax.ShapeDtypeStruct((n, d), jnp.float32),
        scratch_shapes=[pltpu.VMEM((TOK_BLOCK, d), jnp.bfloat16),
                        pltpu.VMEM((d, d), jnp.bfloat16),
                        pltpu.VMEM((d, d), jnp.bfloat16),
                        pltpu.VMEM((pd, d), jnp.bfloat16)],
        compiler_params=pltpu.CompilerParams(
            dimension_semantics=("arbitrary",),
            vmem_limit_bytes=VMEM_LIMIT),
        name="out_proj",
    )(x2, ya2, yb2, p2, w_out, ple_g, w_gate, w_proj, fin_g)


@jax.jit
def kernel(x, p, norm_g, w_in, sink_a, rel_bias_b, w_out, ple_norm_g, w_ple_proj, w_ple_gate,
           final_norm_g):
    b, s, d = x.shape
    assert norm_g.shape[0] == 1, "the final RMSNorm is fused into the (single) layer's output kernel"
    zqa, zqb, zkva, zkvb = _in_proj(x, norm_g[0][None], w_in[0])
    sink = (sink_a[0].astype(jnp.float32) * LOG2E).reshape(A_HEADS // 2, 1, 2, 1)
    sink = jnp.broadcast_to(sink, (A_HEADS // 2, PAIR, 2, HEAD_DIM)).reshape(A_HEADS // 2, PAIR, LANES)
    ya = _attention_a(zqa, zkva, sink)
    yb = _attention_b(zqb, zkvb, _toeplitz_base_rows(rel_bias_b[0]))
    out = _out_proj(x.reshape(b * s, d), ya.reshape(b * s, -1), yb.reshape(b * s, -1),
                    p[0].reshape(b * s, -1), w_out[0], ple_norm_g[0][None],
                    w_ple_gate[0], w_ple_proj[0], final_norm_g[None])
    return out.reshape(b, s, d)
```

```python
import math

import jax
import jax.numpy as jnp
from jax import lax
from jax.experimental import pallas as pl
from jax.experimental.pallas import tpu as pltpu

D_MODEL = 1024
CHUNK = 64
HEAD_DIM = 64
A_HEADS = 8
A_KV_HEADS = 2
A_GROUP = A_HEADS // A_KV_HEADS
B_HEADS = 8
A_PREV = 2
B_PREV = 8
MAX_REL = 128
RMS_EPS = 1e-6
NEG_BIG = -1e30
LOG2E = math.log2(math.e)

A_Q = A_HEADS * HEAD_DIM
A_KV = A_KV_HEADS * HEAD_DIM
A_W = 512
B_W = 512
OFF_QA, OFF_KA, OFF_VA, OFF_GA = 0, 512, 640, 768
OFF_QB, OFF_KB, OFF_VB, OFF_GB = 1280, 1792, 2304, 2816
D_IN_PROJ = 3328

LANES = 128
PAIR = 2 * CHUNK
A_BAND = (A_PREV + 2) * CHUNK
B_BAND = (B_PREV + 2) * CHUNK
TOK_BLOCK = 1024
IN_SUB_BLOCK = 256
OUT_SUB_BLOCK = 256
TOEP_W = B_BAND + PAIR
ZQ_W = 1024
ZKVA_W = 768
ZKVB_W = 1536
PIPE_UNROLL = 4
A_PIPE_LAG = 1
B_PIPE_LAG = 2
VMEM_LIMIT = 56 * 1024 * 1024
Q_SCALE = HEAD_DIM ** -0.5 * LOG2E
NT_DIMS = (((1,), (1,)), ((), ()))


def _rms_scale(xf):
    var = jnp.mean(xf * xf, axis=-1, keepdims=True)
    return xf * lax.rsqrt(var + RMS_EPS)


def _silu(g):
    return g * (1.0 / (1.0 + jnp.exp(-g)))


def _dup_halves(v):
    rolled = pltpu.roll(v, HEAD_DIM, axis=1)
    lane = lax.broadcasted_iota(jnp.int32, v.shape, 1)
    lo = lane < HEAD_DIM
    return jnp.where(lo, v, rolled), jnp.where(lo, rolled, v)


def _cast_weight_once(is_first_step, w_f32_ref, w_ref):
    @pl.when(is_first_step)
    def _():
        for c0 in range(0, w_f32_ref.shape[1], 2 * LANES):
            cols = slice(c0, c0 + 2 * LANES)
            w_ref[:, cols] = w_f32_ref[:, cols].astype(w_ref.dtype)


def _in_proj_kernel(x_ref, g_ref, w_f32_ref, zqa_ref, zqb_ref, zkva_ref, zkvb_ref, w_ref):
    _cast_weight_once((pl.program_id(0) == 0) & (pl.program_id(1) == 0), w_f32_ref, w_ref)

    def project_rows(rows):
        u = (_rms_scale(x_ref[0, rows, :]) * g_ref[...]).astype(jnp.bfloat16)

        def proj(off, width):
            return jnp.dot(u, w_ref[:, off:off + width], preferred_element_type=jnp.float32)

        bf = jnp.bfloat16
        zqa_ref[0, rows, 0:512] = (proj(OFF_QA, A_Q) * Q_SCALE).astype(bf)
        zqa_ref[0, rows, 512:1024] = _silu(proj(OFF_GA, A_W)).astype(bf)
        zqb_ref[0, rows, 0:512] = (proj(OFF_QB, B_W) * Q_SCALE).astype(bf)
        zqb_ref[0, rows, 512:1024] = _silu(proj(OFF_GB, B_W)).astype(bf)
        assert OFF_VA == OFF_KA + A_KV
        kv = proj(OFF_KA, 2 * A_KV)
        kd0, kd1 = _dup_halves(kv[:, :A_KV])
        vd0, vd1 = _dup_halves(kv[:, A_KV:])
        first = lax.broadcasted_iota(jnp.int32, vd0.shape, 1) < HEAD_DIM
        zkva_ref[0, rows, 0:128] = kd0.astype(bf)
        zkva_ref[0, rows, 128:256] = kd1.astype(bf)
        for g, vd in enumerate((vd0, vd1)):
            zkva_ref[0, rows, 256 + 256 * g:384 + 256 * g] = jnp.where(first, vd, 0.0).astype(bf)
            zkva_ref[0, rows, 384 + 256 * g:512 + 256 * g] = jnp.where(first, 0.0, vd).astype(bf)
        zkvb_ref[0, rows, 0:512] = proj(OFF_KB, B_W).astype(bf)
        vb = proj(OFF_VB, B_W)
        even_head = lax.broadcasted_iota(jnp.int32, vb.shape, 1) % LANES < HEAD_DIM
        zkvb_ref[0, rows, 512:1024] = jnp.where(even_head, vb, 0.0).astype(bf)
        zkvb_ref[0, rows, 1024:1536] = jnp.where(even_head, 0.0, vb).astype(bf)

    for r0 in range(0, TOK_BLOCK, IN_SUB_BLOCK):
        project_rows(slice(r0, r0 + IN_SUB_BLOCK))


def _in_proj(x, norm_g, w_in):
    b, s, d = x.shape
    nblk = s // TOK_BLOCK
    assert nblk * TOK_BLOCK == s
    tok_idx = lambda bi, si: (bi, si, 0)
    bf = jnp.bfloat16
    return pl.pallas_call(
        _in_proj_kernel,
        grid=(b, nblk),
        in_specs=[
            pl.BlockSpec((1, TOK_BLOCK, d), tok_idx),
            pl.BlockSpec((1, d), lambda bi, si: (0, 0)),
            pl.BlockSpec((d, D_IN_PROJ), lambda bi, si: (0, 0), pipeline_mode=pl.Buffered(1)),
        ],
        out_specs=[
            pl.BlockSpec((1, TOK_BLOCK, ZQ_W), tok_idx),
            pl.BlockSpec((1, TOK_BLOCK, ZQ_W), tok_idx),
            pl.BlockSpec((1, TOK_BLOCK, ZKVA_W), tok_idx),
            pl.BlockSpec((1, TOK_BLOCK, ZKVB_W), tok_idx),
        ],
        out_shape=[
            jax.ShapeDtypeStruct((b, s, ZQ_W), bf),
            jax.ShapeDtypeStruct((b, s, ZQ_W), bf),
            jax.ShapeDtypeStruct((b, s, ZKVA_W), bf),
            jax.ShapeDtypeStruct((b, s, ZKVB_W), bf),
        ],
        scratch_shapes=[pltpu.VMEM((d, D_IN_PROJ), bf)],
        compiler_params=pltpu.CompilerParams(
            dimension_semantics=("arbitrary", "arbitrary"),
            vmem_limit_bytes=VMEM_LIMIT),
        name="in_proj",
    )(x, norm_g, w_in)


def _band_geometry(width, n_prev):
    i = lax.broadcasted_iota(jnp.int32, (PAIR, width), 0)
    jk = lax.broadcasted_iota(jnp.int32, (PAIR, width), 1)
    ci, kc = i // CHUNK, jk // CHUNK
    return i, jk, (kc >= ci) & (kc <= ci + n_prev)


def _software_pipeline(n, units, lag, n_cut, min_fill=0):
    u = PIPE_UNROLL
    fill = max(-(-(n_cut + 2 * lag) // u) * u, min_fill)
    assert n % u == 0 and n >= fill and 2 * lag <= u

    def step(base, j, stage_range):
        for first, second, third in units:
            if 2 in stage_range:
                third(base + j - 2 * lag, (j - 2 * lag) % u)
            if 0 in stage_range:
                first(base + j, j % u)
            if 1 in stage_range:
                second(base + j - lag, (j - lag) % u)

    for j in range(fill):
        step(0, j, [k for k in range(3) if j >= k * lag])

    def steady(t, carry):
        for j in range(u):
            step(t * u, j, range(3))
        return carry

    if fill < n:
        lax.fori_loop(fill // u, n // u, steady, 0)
    for j in range(u, u + 2 * lag):
        step(n - u, j, [k for k in range(1, 3) if j - k * lag < u])


def _head_masks():
    lane = lax.broadcasted_iota(jnp.int32, (PAIR, LANES), 1)
    return lane < HEAD_DIM


def _head_ones_tiles():
    pack_rows = 16
    lo_tile = lax.broadcasted_iota(jnp.int32, (pack_rows, LANES), 1) < HEAD_DIM
    one_lo = jnp.where(lo_tile, 1.0, 0.0).astype(jnp.bfloat16)
    one_hi = jnp.where(lo_tile, 0.0, 1.0).astype(jnp.bfloat16)
    return pack_rows, one_lo, one_hi


def _split_heads(qp, lo):
    zero = jnp.zeros((), qp.dtype)
    return jnp.concatenate([jnp.where(lo, qp, zero), jnp.where(lo, zero, qp)], axis=0)


def _merge_heads(o, lo):
    return jnp.where(lo, o[0:PAIR], o[PAIR:2 * PAIR])


def _pair_start(pp):
    return pp * PAIR if isinstance(pp, int) else pl.multiple_of(pp * PAIR, PAIR)


def _q_rows(pp):
    return pl.ds(_pair_start(pp), PAIR)


def _band(pp, n_prev, width):
    if isinstance(pp, int):
        start = pp * PAIR - n_prev * CHUNK
        cut = max(0, -start)
        return pl.ds(start + cut, width - cut), cut
    assert (n_prev * CHUNK) % PAIR == 0
    return pl.ds(pl.multiple_of(pp * PAIR - n_prev * CHUNK, PAIR), width), 0


def _attn_a_kernel(zq_ref, zkv_ref, sink_ref, y_ref, bias_ref, s_ref, m_ref, p_ref, t_ref):
    n_pairs = zq_ref.shape[1] // PAIR

    @pl.when(pl.program_id(0) == 0)
    def _():
        i, jk, valid = _band_geometry(A_BAND, A_PREV)
        dist = jnp.abs(i - jk + A_PREV * CHUNK).astype(jnp.float32)
        for h in range(A_HEADS):
            slope = 2.0 ** (-8.0 * (h + 1) / A_HEADS) * LOG2E
            rows = slice((h % A_GROUP) * PAIR, (h % A_GROUP + 1) * PAIR)
            bias_ref[h // A_GROUP, rows, :] = jnp.where(valid, -slope * dist, NEG_BIG)

    lo = _head_masks()
    pack_rows, one_lo, one_hi = _head_ones_tiles()

    def unit(hp):
        g = hp // (A_GROUP // 2)
        bias_rows = slice((hp % (A_GROUP // 2)) * 2 * PAIR, (hp % (A_GROUP // 2) + 1) * 2 * PAIR)

        def scores(pp, slot):
            qp = zq_ref[0, _q_rows(pp), 128 * hp:128 * hp + 128]
            band, cut = _band(pp, A_PREV, A_BAND)
            k = zkv_ref[0, band, 128 * g:128 * g + 128]
            s = lax.dot_general(_split_heads(qp, lo), k, NT_DIMS,
                                preferred_element_type=jnp.float32)
            s = s + bias_ref[g, bias_rows, cut:]
            s_ref[hp, slot, :, cut:] = s
            m = jnp.max(s, axis=-1, keepdims=True)
            m_ref[hp, slot] = jnp.broadcast_to(m, (2 * PAIR, LANES))

        def weights(pp, slot):
            _, cut = _band(pp, A_PREV, A_BAND)
            m = m_ref[hp, slot]
            for t in range(cut // LANES, A_BAND // LANES):
                cols = slice(t * LANES, (t + 1) * LANES)
                e = jnp.exp2(s_ref[hp, slot, :, cols] - m).astype(jnp.bfloat16)
                p_ref[hp, slot, :, cols] = e[0:PAIR]
                p_ref[hp, slot, :, A_BAND + t * LANES:A_BAND + (t + 1) * LANES] = e[PAIR:2 * PAIR]
            t_ref[hp, slot] = jnp.exp2(sink_ref[hp] - _merge_heads(m, lo))

        def output(pp, slot):
            rows = _q_rows(pp)
            band, cut = _band(pp, A_PREV, A_BAND)
            v_first = zkv_ref[0, band, 256 + 256 * g:256 + 256 * g + 128]
            v_second = zkv_ref[0, band, 384 + 256 * g:384 + 256 * g + 128]
            e = jnp.concatenate([p_ref[hp, slot, :, cut:A_BAND], p_ref[hp, slot, :, A_BAND + cut:]],
                                axis=1)
            n_tiles = (A_BAND - cut) // pack_rows
            rhs = jnp.concatenate([
                jnp.concatenate([v_first, jnp.concatenate([one_lo] * n_tiles, axis=0)], axis=1),
                jnp.concatenate([v_second, jnp.concatenate([one_hi] * n_tiles, axis=0)], axis=1),
            ], axis=0)
            o = jnp.dot(e, rhs, preferred_element_type=jnp.float32)
            gate = zq_ref[0, rows, 512 + 128 * hp:512 + 128 * hp + 128].astype(jnp.float32)
            y_ref[0, rows, 128 * hp:128 * hp + 128] = (
                o[:, :LANES] / (o[:, LANES:] + t_ref[hp, slot]) * gate).astype(y_ref.dtype)

        return scores, weights, output

    _software_pipeline(n_pairs, [unit(hp) for hp in range(A_HEADS // 2)], A_PIPE_LAG,
                       -(-A_PREV * CHUNK // PAIR), min_fill=n_pairs)


def _attention_a(zq, zkv, sink):
    b, s, _ = zq.shape
    slots = PIPE_UNROLL
    units = A_HEADS // 2
    return pl.pallas_call(
        _attn_a_kernel,
        grid=(b,),
        in_specs=[
            pl.BlockSpec((1, s, ZQ_W), lambda bi: (bi, 0, 0)),
            pl.BlockSpec((1, s, ZKVA_W), lambda bi: (bi, 0, 0)),
            pl.BlockSpec(sink.shape, lambda bi: (0, 0, 0)),
        ],
        out_specs=pl.BlockSpec((1, s, A_W), lambda bi: (bi, 0, 0)),
        out_shape=jax.ShapeDtypeStruct((b, s, A_W), jnp.bfloat16),
        scratch_shapes=[
            pltpu.VMEM((A_KV_HEADS, A_GROUP * PAIR, A_BAND), jnp.float32),
            pltpu.VMEM((units, slots, 2 * PAIR, A_BAND), jnp.float32),
            pltpu.VMEM((units, slots, 2 * PAIR, LANES), jnp.float32),
            pltpu.VMEM((units, slots, PAIR, 2 * A_BAND), jnp.bfloat16),
            pltpu.VMEM((units, slots, PAIR, LANES), jnp.float32),
        ],
        compiler_params=pltpu.CompilerParams(
            dimension_semantics=("arbitrary",),
            vmem_limit_bytes=VMEM_LIMIT),
        name="attention_a",
    )(zq, zkv, sink)


def _attn_b_kernel(zq_ref, zkv_ref, b0_ref, y_ref, bias_ref, s_ref, m_ref, p_ref):
    n_pairs = zq_ref.shape[1] // PAIR

    @pl.when(pl.program_id(0) == 0)
    def _():
        _, _, valid = _band_geometry(B_BAND, B_PREV)
        for h in range(B_HEADS):
            row = jnp.broadcast_to(b0_ref[h:h + 1, :], (PAIR, TOEP_W))
            toep = pltpu.roll(row, 0, axis=1, stride=1, stride_axis=0)[:, :B_BAND]
            rows = slice((h % 2) * PAIR, (h % 2 + 1) * PAIR)
            bias_ref[h // 2, rows, :] = jnp.where(valid, toep * LOG2E, NEG_BIG)

    lo = _head_masks()
    pack_rows, one_lo, one_hi = _head_ones_tiles()

    def unit(hp):
        def scores(pp, slot):
            qp = zq_ref[0, _q_rows(pp), 128 * hp:128 * hp + 128]
            band, cut = _band(pp, B_PREV, B_BAND)
            k = zkv_ref[0, band, 128 * hp:128 * hp + 128]
            s = lax.dot_general(_split_heads(qp, lo), k, NT_DIMS,
                                preferred_element_type=jnp.float32)
            s = s + bias_ref[hp, :, cut:]
            s_ref[hp, slot, :, cut:] = s
            m = jnp.max(s, axis=-1, keepdims=True)
            m_ref[hp, slot] = jnp.broadcast_to(m, (2 * PAIR, LANES))

        def weights(pp, slot):
            _, cut = _band(pp, B_PREV, B_BAND)
            m = m_ref[hp, slot]
            for t in range(cut // LANES, B_BAND // LANES):
                cols = slice(t * LANES, (t + 1) * LANES)
                e = jnp.exp2(s_ref[hp, slot, :, cols] - m).astype(jnp.bfloat16)
                p_ref[hp, slot, :, cols] = e[0:PAIR]
                p_ref[hp, slot, :, B_BAND + t * LANES:B_BAND + (t + 1) * LANES] = e[PAIR:2 * PAIR]

        def output(pp, slot):
            rows = _q_rows(pp)
            band, cut = _band(pp, B_PREV, B_BAND)
            v_h0 = zkv_ref[0, band, 512 + 128 * hp:512 + 128 * hp + 128]
            v_h1 = zkv_ref[0, band, 1024 + 128 * hp:1024 + 128 * hp + 128]
            e = jnp.concatenate([p_ref[hp, slot, :, cut:B_BAND], p_ref[hp, slot, :, B_BAND + cut:]],
                                axis=1)
            n_tiles = (B_BAND - cut) // pack_rows
            rhs = jnp.concatenate([
                jnp.concatenate([v_h0, jnp.concatenate([one_lo] * n_tiles, axis=0)], axis=1),
                jnp.concatenate([v_h1, jnp.concatenate([one_hi] * n_tiles, axis=0)], axis=1),
            ], axis=0)
            o = jnp.dot(e, rhs, preferred_element_type=jnp.float32)
            gate = zq_ref[0, rows, 512 + 128 * hp:512 + 128 * hp + 128].astype(jnp.float32)
            y_ref[0, rows, 128 * hp:128 * hp + 128] = (
                o[:, :LANES] / o[:, LANES:] * gate).astype(y_ref.dtype)

        return scores, weights, output

    _software_pipeline(n_pairs, [unit(hp) for hp in range(B_HEADS // 2)], B_PIPE_LAG,
                       -(-B_PREV * CHUNK // PAIR))


def _toeplitz_base_rows(rel_bias_b):
    tbl = rel_bias_b.astype(jnp.float32)
    h = tbl.shape[0]
    far = tbl[:, 2 * MAX_REL:2 * MAX_REL + 1]
    n_far = B_PREV * CHUNK - MAX_REL
    n_mid = B_BAND - n_far
    assert n_mid <= 2 * MAX_REL
    mid = tbl[:, 2 * MAX_REL:2 * MAX_REL - n_mid:-1]
    return jnp.concatenate(
        [jnp.broadcast_to(far, (h, n_far)), mid, jnp.broadcast_to(far, (h, TOEP_W - B_BAND))], axis=1)


def _attention_b(zq, zkv, b0):
    b, s, _ = zq.shape
    slots = PIPE_UNROLL
    units = B_HEADS // 2
    return pl.pallas_call(
        _attn_b_kernel,
        grid=(b,),
        in_specs=[
            pl.BlockSpec((1, s, ZQ_W), lambda bi: (bi, 0, 0)),
            pl.BlockSpec((1, s, ZKVB_W), lambda bi: (bi, 0, 0)),
            pl.BlockSpec(b0.shape, lambda bi: (0, 0)),
        ],
        out_specs=pl.BlockSpec((1, s, B_W), lambda bi: (bi, 0, 0)),
        out_shape=jax.ShapeDtypeStruct((b, s, B_W), jnp.bfloat16),
        scratch_shapes=[
            pltpu.VMEM((units, 2 * PAIR, B_BAND), jnp.float32),
            pltpu.VMEM((units, slots, 2 * PAIR, B_BAND), jnp.float32),
            pltpu.VMEM((units, slots, 2 * PAIR, LANES), jnp.float32),
            pltpu.VMEM((units, slots, PAIR, 2 * B_BAND), jnp.bfloat16),
        ],
        compiler_params=pltpu.CompilerParams(
            dimension_semantics=("arbitrary",),
            vmem_limit_bytes=VMEM_LIMIT),
        name="attention_b",
    )(zq, zkv, b0)


def _out_proj_kernel(x_ref, ya_ref, yb_ref, p_ref, w_out_f32_ref, ple_g_ref, w_gate_f32_ref,
                     w_proj_f32_ref, fin_g_ref, o_ref, u_ref, w_out_ref, w_gate_ref, w_proj_ref):
    wa = ya_ref.shape[1]
    first_step = pl.program_id(0) == 0
    _cast_weight_once(first_step, w_out_f32_ref, w_out_ref)
    _cast_weight_once(first_step, w_gate_f32_ref, w_gate_ref)
    _cast_weight_once(first_step, w_proj_f32_ref, w_proj_ref)

    def residual(rows):
        h = (x_ref[rows, :]
             + jnp.dot(ya_ref[rows, :], w_out_ref[:wa, :], preferred_element_type=jnp.float32)
             + jnp.dot(yb_ref[rows, :], w_out_ref[wa:, :], preferred_element_type=jnp.float32))
        o_ref[rows, :] = h
        u_ref[rows, :] = (_rms_scale(h) * ple_g_ref[...]).astype(jnp.bfloat16)

    def embed(rows):
        gl = jnp.dot(u_ref[rows, :], w_gate_ref[...], preferred_element_type=jnp.float32)
        gate = 1.0 / (1.0 + jnp.exp(-gl))
        pe = jnp.dot(p_ref[rows, :].astype(jnp.bfloat16), w_proj_ref[...],
                     preferred_element_type=jnp.float32)
        h = o_ref[rows, :] + pe * gate
        o_ref[rows, :] = _rms_scale(h) * fin_g_ref[...]

    subs = [slice(r0, r0 + OUT_SUB_BLOCK) for r0 in range(0, TOK_BLOCK, OUT_SUB_BLOCK)]
    for rows in subs:
        residual(rows)
    for rows in subs:
        embed(rows)


def _out_proj(x2, ya2, yb2, p2, w_out, ple_g, w_gate, w_proj, fin_g):
    n, d = x2.shape
    pd = p2.shape[1]
    tok = lambda i: (i, 0)
    const = lambda i: (0, 0)
    return pl.pallas_call(
        _out_proj_kernel,
        grid=(n // TOK_BLOCK,),
        in_specs=[
            pl.BlockSpec((TOK_BLOCK, d), tok),
            pl.BlockSpec((TOK_BLOCK, ya2.shape[1]), tok),
            pl.BlockSpec((TOK_BLOCK, yb2.shape[1]), tok),
            pl.BlockSpec((TOK_BLOCK, pd), tok),
            pl.BlockSpec((d, d), const, pipeline_mode=pl.Buffered(1)),
            pl.BlockSpec((1, d), const),
            pl.BlockSpec((d, d), const, pipeline_mode=pl.Buffered(1)),
            pl.BlockSpec((pd, d), const, pipeline_mode=pl.Buffered(1)),
            pl.BlockSpec((1, d), const),
        ],
        out_specs=pl.BlockSpec((TOK_BLOCK, d), tok),
        out_shape=jax.ShapeDtypeStruct((n, d), jnp.float32),
        scratch_shapes=[pltpu.VMEM((TOK_BLOCK, d), jnp.bfloat16),
                        pltpu.VMEM((d, d), jnp.bfloat16),
                        pltpu.VMEM((d, d), jnp.bfloat16),
                        pltpu.VMEM((pd, d), jnp.bfloat16)],
        compiler_params=pltpu.CompilerParams(
            dimension_semantics=("arbitrary",),
            vmem_limit_bytes=VMEM_LIMIT),
        name="out_proj",
    )(x2, ya2, yb2, p2, w_out, ple_g, w_gate, w_proj, fin_g)


@jax.jit
def kernel(x, p, norm_g, w_in, sink_a, rel_bias_b, w_out, ple_norm_g, w_ple_proj, w_ple_gate,
           final_norm_g):
    b, s, d = x.shape
    assert norm_g.shape[0] == 1, "the final RMSNorm is fused into the (single) layer's output kernel"
    zqa, zqb, zkva, zkvb = _in_proj(x, norm_g[0][None], w_in[0])
    sink = (sink_a[0].astype(jnp.float32) * LOG2E).reshape(A_HEADS // 2, 1, 2, 1)
    sink = jnp.broadcast_to(sink, (A_HEADS // 2, PAIR, 2, HEAD_DIM)).reshape(A_HEADS // 2, PAIR, LANES)
    ya = _attention_a(zqa, zkva, sink)
    yb = _attention_b(zqb, zkvb, _toeplitz_base_rows(rel_bias_b[0]))
    out = _out_proj(x.reshape(b * s, d), ya.reshape(b * s, -1), yb.reshape(b * s, -1),
                    p[0].reshape(b * s, -1), w_out[0], ple_norm_g[0][None],
                    w_ple_gate[0], w_ple_proj[0], final_norm_g[None])
    return out.reshape(b, s, d)
```

```python
import math

import jax
import jax.numpy as jnp
from jax import lax
from jax.experimental import pallas as pl
from jax.experimental.pallas import tpu as pltpu

D_MODEL = 1024
CHUNK = 64
HEAD_DIM = 64
A_HEADS = 8
A_KV_HEADS = 2
A_GROUP = A_HEADS // A_KV_HEADS
B_HEADS = 8
A_PREV = 2
B_PREV = 8
MAX_REL = 128
RMS_EPS = 1e-6
NEG_BIG = -1e30
LOG2E = math.log2(math.e)

A_Q = A_HEADS * HEAD_DIM
A_KV = A_KV_HEADS * HEAD_DIM
A_W = 512
B_W = 512
OFF_QA, OFF_KA, OFF_VA, OFF_GA = 0, 512, 640, 768
OFF_QB, OFF_KB, OFF_VB, OFF_GB = 1280, 1792, 2304, 2816
D_IN_PROJ = 3328

LANES = 128
PAIR = 2 * CHUNK
A_BAND = (A_PREV + 2) * CHUNK
B_BAND = (B_PREV + 2) * CHUNK
TOK_BLOCK = 1024
IN_SUB_BLOCK = 256
OUT_SUB_BLOCK = 256
TOEP_W = B_BAND + PAIR
ZQ_W = 1024
ZKVA_W = 768
ZKVB_W = 1536
PIPE_UNROLL = 4
A_SEQS_PER_STEP = 2
A_PIPE_LAG = 1
B_PIPE_LAG = 2
VMEM_LIMIT = 56 * 1024 * 1024
Q_SCALE = HEAD_DIM ** -0.5 * LOG2E
NT_DIMS = (((1,), (1,)), ((), ()))


def _rms_scale(xf):
    var = jnp.mean(xf * xf, axis=-1, keepdims=True)
    return xf * lax.rsqrt(var + RMS_EPS)


def _silu(g):
    return g * (1.0 / (1.0 + jnp.exp(-g)))


def _dup_halves(v):
    rolled = pltpu.roll(v, HEAD_DIM, axis=1)
    lane = lax.broadcasted_iota(jnp.int32, v.shape, 1)
    lo = lane < HEAD_DIM
    return jnp.where(lo, v, rolled), jnp.where(lo, rolled, v)


def _cast_weight_once(is_first_step, w_f32_ref, w_ref):
    @pl.when(is_first_step)
    def _():
        for c0 in range(0, w_f32_ref.shape[1], 2 * LANES):
            cols = slice(c0, c0 + 2 * LANES)
            w_ref[:, cols] = w_f32_ref[:, cols].astype(w_ref.dtype)


def _in_proj_kernel(x_ref, g_ref, w_f32_ref, zqa_ref, zqb_ref, zkva_ref, zkvb_ref, w_ref):
    _cast_weight_once((pl.program_id(0) == 0) & (pl.program_id(1) == 0), w_f32_ref, w_ref)

    def project_rows(rows):
        u = (_rms_scale(x_ref[0, rows, :]) * g_ref[...]).astype(jnp.bfloat16)

        def proj(off, width):
            return jnp.dot(u, w_ref[:, off:off + width], preferred_element_type=jnp.float32)

        bf = jnp.bfloat16
        zqa_ref[0, rows, 0:512] = (proj(OFF_QA, A_Q) * Q_SCALE).astype(bf)
        zqa_ref[0, rows, 512:1024] = _silu(proj(OFF_GA, A_W)).astype(bf)
        zqb_ref[0, rows, 0:512] = (proj(OFF_QB, B_W) * Q_SCALE).astype(bf)
        zqb_ref[0, rows, 512:1024] = _silu(proj(OFF_GB, B_W)).astype(bf)
        assert OFF_VA == OFF_KA + A_KV
        kv = proj(OFF_KA, 2 * A_KV)
        kd0, kd1 = _dup_halves(kv[:, :A_KV])
        vd0, vd1 = _dup_halves(kv[:, A_KV:])
        first = lax.broadcasted_iota(jnp.int32, vd0.shape, 1) < HEAD_DIM
        zkva_ref[0, rows, 0:128] = kd0.astype(bf)
        zkva_ref[0, rows, 128:256] = kd1.astype(bf)
        for g, vd in enumerate((vd0, vd1)):
            zkva_ref[0, rows, 256 + 256 * g:384 + 256 * g] = jnp.where(first, vd, 0.0).astype(bf)
            zkva_ref[0, rows, 384 + 256 * g:512 + 256 * g] = jnp.where(first, 0.0, vd).astype(bf)
        zkvb_ref[0, rows, 0:512] = proj(OFF_KB, B_W).astype(bf)
        vb = proj(OFF_VB, B_W)
        even_head = lax.broadcasted_iota(jnp.int32, vb.shape, 1) % LANES < HEAD_DIM
        zkvb_ref[0, rows, 512:1024] = jnp.where(even_head, vb, 0.0).astype(bf)
        zkvb_ref[0, rows, 1024:1536] = jnp.where(even_head, 0.0, vb).astype(bf)

    for r0 in range(0, TOK_BLOCK, IN_SUB_BLOCK):
        project_rows(slice(r0, r0 + IN_SUB_BLOCK))


def _in_proj(x, norm_g, w_in):
    b, s, d = x.shape
    nblk = s // TOK_BLOCK
    assert nblk * TOK_BLOCK == s
    tok_idx = lambda bi, si: (bi, si, 0)
    bf = jnp.bfloat16
    return pl.pallas_call(
        _in_proj_kernel,
        grid=(b, nblk),
        in_specs=[
            pl.BlockSpec((1, TOK_BLOCK, d), tok_idx),
            pl.BlockSpec((1, d), lambda bi, si: (0, 0)),
            pl.BlockSpec((d, D_IN_PROJ), lambda bi, si: (0, 0), pipeline_mode=pl.Buffered(1)),
        ],
        out_specs=[
            pl.BlockSpec((1, TOK_BLOCK, ZQ_W), tok_idx),
            pl.BlockSpec((1, TOK_BLOCK, ZQ_W), tok_idx),
            pl.BlockSpec((1, TOK_BLOCK, ZKVA_W), tok_idx),
            pl.BlockSpec((1, TOK_BLOCK, ZKVB_W), tok_idx),
        ],
        out_shape=[
            jax.ShapeDtypeStruct((b, s, ZQ_W), bf),
            jax.ShapeDtypeStruct((b, s, ZQ_W), bf),
            jax.ShapeDtypeStruct((b, s, ZKVA_W), bf),
            jax.ShapeDtypeStruct((b, s, ZKVB_W), bf),
        ],
        scratch_shapes=[pltpu.VMEM((d, D_IN_PROJ), bf)],
        compiler_params=pltpu.CompilerParams(
            dimension_semantics=("arbitrary", "arbitrary"),
            vmem_limit_bytes=VMEM_LIMIT),
        name="in_proj",
    )(x, norm_g, w_in)


def _band_geometry(width, n_prev):
    i = lax.broadcasted_iota(jnp.int32, (PAIR, width), 0)
    jk = lax.broadcasted_iota(jnp.int32, (PAIR, width), 1)
    ci, kc = i // CHUNK, jk // CHUNK
    return i, jk, (kc >= ci) & (kc <= ci + n_prev)


def _software_pipeline(n, units, lag, n_cut, min_fill=0):
    u = PIPE_UNROLL
    fill = max(-(-(n_cut + 2 * lag) // u) * u, min_fill)
    assert n % u == 0 and n >= fill and 2 * lag <= u

    def step(base, j, stage_range):
        for first, second, third in units:
            if 2 in stage_range:
                third(base + j - 2 * lag, (j - 2 * lag) % u)
            if 0 in stage_range:
                first(base + j, j % u)
            if 1 in stage_range:
                second(base + j - lag, (j - lag) % u)

    for j in range(fill):
        step(0, j, [k for k in range(3) if j >= k * lag])

    def steady(t, carry):
        for j in range(u):
            step(t * u, j, range(3))
        return carry

    if fill < n:
        lax.fori_loop(fill // u, n // u, steady, 0)
    for j in range(u, u + 2 * lag):
        step(n - u, j, [k for k in range(1, 3) if j - k * lag < u])


def _head_masks():
    lane = lax.broadcasted_iota(jnp.int32, (PAIR, LANES), 1)
    return lane < HEAD_DIM


def _head_ones_tiles():
    pack_rows = 16
    lo_tile = lax.broadcasted_iota(jnp.int32, (pack_rows, LANES), 1) < HEAD_DIM
    one_lo = jnp.where(lo_tile, 1.0, 0.0).astype(jnp.bfloat16)
    one_hi = jnp.where(lo_tile, 0.0, 1.0).astype(jnp.bfloat16)
    return pack_rows, one_lo, one_hi


def _split_heads(qp, lo):
    zero = jnp.zeros((), qp.dtype)
    return jnp.concatenate([jnp.where(lo, qp, zero), jnp.where(lo, zero, qp)], axis=0)


def _merge_heads(o, lo):
    return jnp.where(lo, o[0:PAIR], o[PAIR:2 * PAIR])


def _pair_start(pp):
    return pp * PAIR if isinstance(pp, int) else pl.multiple_of(pp * PAIR, PAIR)


def _q_rows(pp):
    return pl.ds(_pair_start(pp), PAIR)


def _band(pp, n_prev, width):
    if isinstance(pp, int):
        start = pp * PAIR - n_prev * CHUNK
        cut = max(0, -start)
        return pl.ds(start + cut, width - cut), cut
    assert (n_prev * CHUNK) % PAIR == 0
    return pl.ds(pl.multiple_of(pp * PAIR - n_prev * CHUNK, PAIR), width), 0


def _attn_a_kernel(zq_ref, zkv_ref, sink_ref, y_ref, bias_ref, s_ref, m_ref, p_ref, t_ref):
    n_pairs = zq_ref.shape[1] // PAIR

    @pl.when(pl.program_id(0) == 0)
    def _():
        i, jk, valid = _band_geometry(A_BAND, A_PREV)
        dist = jnp.abs(i - jk + A_PREV * CHUNK).astype(jnp.float32)
        for h in range(A_HEADS):
            slope = 2.0 ** (-8.0 * (h + 1) / A_HEADS) * LOG2E
            rows = slice((h % A_GROUP) * PAIR, (h % A_GROUP + 1) * PAIR)
            bias_ref[h // A_GROUP, rows, :] = jnp.where(valid, -slope * dist, NEG_BIG)

    lo = _head_masks()
    pack_rows, one_lo, one_hi = _head_ones_tiles()

    def unit(seq, hp):
        g = hp // (A_GROUP // 2)
        bias_rows = slice((hp % (A_GROUP // 2)) * 2 * PAIR, (hp % (A_GROUP // 2) + 1) * 2 * PAIR)
        un = seq * (A_HEADS // 2) + hp

        def scores(pp, slot):
            qp = zq_ref[seq, _q_rows(pp), 128 * hp:128 * hp + 128]
            band, cut = _band(pp, A_PREV, A_BAND)
            k = zkv_ref[seq, band, 128 * g:128 * g + 128]
            s = lax.dot_general(_split_heads(qp, lo), k, NT_DIMS,
                                preferred_element_type=jnp.float32)
            s = s + bias_ref[g, bias_rows, cut:]
            s_ref[un, slot, :, cut:] = s
            m = jnp.max(s, axis=-1, keepdims=True)
            m_ref[un, slot] = jnp.broadcast_to(m, (2 * PAIR, LANES))

        def weights(pp, slot):
            _, cut = _band(pp, A_PREV, A_BAND)
            m = m_ref[un, slot]
            for t in range(cut // LANES, A_BAND // LANES):
                cols = slice(t * LANES, (t + 1) * LANES)
                e = jnp.exp2(s_ref[un, slot, :, cols] - m).astype(jnp.bfloat16)
                p_ref[un, slot, :, cols] = e[0:PAIR]
                p_ref[un, slot, :, A_BAND + t * LANES:A_BAND + (t + 1) * LANES] = e[PAIR:2 * PAIR]
            t_ref[un, slot] = jnp.exp2(sink_ref[hp] - _merge_heads(m, lo))

        def output(pp, slot):
            rows = _q_rows(pp)
            band, cut = _band(pp, A_PREV, A_BAND)
            v_first = zkv_ref[seq, band, 256 + 256 * g:256 + 256 * g + 128]
            v_second = zkv_ref[seq, band, 384 + 256 * g:384 + 256 * g + 128]
            e = jnp.concatenate([p_ref[un, slot, :, cut:A_BAND], p_ref[un, slot, :, A_BAND + cut:]],
                                axis=1)
            n_tiles = (A_BAND - cut) // pack_rows
            rhs = jnp.concatenate([
                jnp.concatenate([v_first, jnp.concatenate([one_lo] * n_tiles, axis=0)], axis=1),
                jnp.concatenate([v_second, jnp.concatenate([one_hi] * n_tiles, axis=0)], axis=1),
            ], axis=0)
            o = jnp.dot(e, rhs, preferred_element_type=jnp.float32)
            gate = zq_ref[seq, rows, 512 + 128 * hp:512 + 128 * hp + 128].astype(jnp.float32)
            y_ref[seq, rows, 128 * hp:128 * hp + 128] = (
                o[:, :LANES] / (o[:, LANES:] + t_ref[un, slot]) * gate).astype(y_ref.dtype)

        return scores, weights, output

    units = [unit(seq, hp) for seq in range(zq_ref.shape[0]) for hp in range(A_HEADS // 2)]
    _software_pipeline(n_pairs, units, A_PIPE_LAG, -(-A_PREV * CHUNK // PAIR), min_fill=n_pairs)


def _attention_a(zq, zkv, sink):
    b, s, _ = zq.shape
    slots = PIPE_UNROLL
    seqs = A_SEQS_PER_STEP
    assert b % seqs == 0
    units = seqs * (A_HEADS // 2)
    return pl.pallas_call(
        _attn_a_kernel,
        grid=(b // seqs,),
        in_specs=[
            pl.BlockSpec((seqs, s, ZQ_W), lambda bi: (bi, 0, 0)),
            pl.BlockSpec((seqs, s, ZKVA_W), lambda bi: (bi, 0, 0)),
            pl.BlockSpec(sink.shape, lambda bi: (0, 0, 0)),
        ],
        out_specs=pl.BlockSpec((seqs, s, A_W), lambda bi: (bi, 0, 0)),
        out_shape=jax.ShapeDtypeStruct((b, s, A_W), jnp.bfloat16),
        scratch_shapes=[
            pltpu.VMEM((A_KV_HEADS, A_GROUP * PAIR, A_BAND), jnp.float32),
            pltpu.VMEM((units, slots, 2 * PAIR, A_BAND), jnp.float32),
            pltpu.VMEM((units, slots, 2 * PAIR, LANES), jnp.float32),
            pltpu.VMEM((units, slots, PAIR, 2 * A_BAND), jnp.bfloat16),
            pltpu.VMEM((units, slots, PAIR, LANES), jnp.float32),
        ],
        compiler_params=pltpu.CompilerParams(
            dimension_semantics=("arbitrary",),
            vmem_limit_bytes=VMEM_LIMIT),
        name="attention_a",
    )(zq, zkv, sink)


def _attn_b_kernel(zq_ref, zkv_ref, b0_ref, y_ref, bias_ref, s_ref, m_ref, p_ref):
    n_pairs = zq_ref.shape[1] // PAIR

    @pl.when(pl.program_id(0) == 0)
    def _():
        _, _, valid = _band_geometry(B_BAND, B_PREV)
        for h in range(B_HEADS):
            row = jnp.broadcast_to(b0_ref[h:h + 1, :], (PAIR, TOEP_W))
            toep = pltpu.roll(row, 0, axis=1, stride=1, stride_axis=0)[:, :B_BAND]
            rows = slice((h % 2) * PAIR, (h % 2 + 1) * PAIR)
            bias_ref[h // 2, rows, :] = jnp.where(valid, toep * LOG2E, NEG_BIG)

    lo = _head_masks()
    pack_rows, one_lo, one_hi = _head_ones_tiles()

    def unit(hp):
        def scores(pp, slot):
            qp = zq_ref[0, _q_rows(pp), 128 * hp:128 * hp + 128]
            band, cut = _band(pp, B_PREV, B_BAND)
            k = zkv_ref[0, band, 128 * hp:128 * hp + 128]
            s = lax.dot_general(_split_heads(qp, lo), k, NT_DIMS,
                                preferred_element_type=jnp.float32)
            s = s + bias_ref[hp, :, cut:]
            s_ref[hp, slot, :, cut:] = s
            m = jnp.max(s, axis=-1, keepdims=True)
            m_ref[hp, slot] = jnp.broadcast_to(m, (2 * PAIR, LANES))

        def weights(pp, slot):
            _, cut = _band(pp, B_PREV, B_BAND)
            m = m_ref[hp, slot]
            for t in range(cut // LANES, B_BAND // LANES):
                cols = slice(t * LANES, (t + 1) * LANES)
                e = jnp.exp2(s_ref[hp, slot, :, cols] - m).astype(jnp.bfloat16)
                p_ref[hp, slot, :, cols] = e[0:PAIR]
                p_ref[hp, slot, :, B_BAND + t * LANES:B_BAND + (t + 1) * LANES] = e[PAIR:2 * PAIR]

        def output(pp, slot):
            rows = _q_rows(pp)
            band, cut = _band(pp, B_PREV, B_BAND)
            v_h0 = zkv_ref[0, band, 512 + 128 * hp:512 + 128 * hp + 128]
            v_h1 = zkv_ref[0, band, 1024 + 128 * hp:1024 + 128 * hp + 128]
            e = jnp.concatenate([p_ref[hp, slot, :, cut:B_BAND], p_ref[hp, slot, :, B_BAND + cut:]],
                                axis=1)
            n_tiles = (B_BAND - cut) // pack_rows
            rhs = jnp.concatenate([
                jnp.concatenate([v_h0, jnp.concatenate([one_lo] * n_tiles, axis=0)], axis=1),
                jnp.concatenate([v_h1, jnp.concatenate([one_hi] * n_tiles, axis=0)], axis=1),
            ], axis=0)
            o = jnp.dot(e, rhs, preferred_element_type=jnp.float32)
            gate = zq_ref[0, rows, 512 + 128 * hp:512 + 128 * hp + 128].astype(jnp.float32)
            y_ref[0, rows, 128 * hp:128 * hp + 128] = (
                o[:, :LANES] / o[:, LANES:] * gate).astype(y_ref.dtype)

        return scores, weights, output

    _software_pipeline(n_pairs, [unit(hp) for hp in range(B_HEADS // 2)], B_PIPE_LAG,
                       -(-B_PREV * CHUNK // PAIR))


def _toeplitz_base_rows(rel_bias_b):
    tbl = rel_bias_b.astype(jnp.float32)
    h = tbl.shape[0]
    far = tbl[:, 2 * MAX_REL:2 * MAX_REL + 1]
    n_far = B_PREV * CHUNK - MAX_REL
    n_mid = B_BAND - n_far
    assert n_mid <= 2 * MAX_REL
    mid = tbl[:, 2 * MAX_REL:2 * MAX_REL - n_mid:-1]
    return jnp.concatenate(
        [jnp.broadcast_to(far, (h, n_far)), mid, jnp.broadcast_to(far, (h, TOEP_W - B_BAND))], axis=1)


def _attention_b(zq, zkv, b0):
    b, s, _ = zq.shape
    slots = PIPE_UNROLL
    units = B_HEADS // 2
    return pl.pallas_call(
        _attn_b_kernel,
        grid=(b,),
        in_specs=[
            pl.BlockSpec((1, s, ZQ_W), lambda bi: (bi, 0, 0)),
            pl.BlockSpec((1, s, ZKVB_W), lambda bi: (bi, 0, 0)),
            pl.BlockSpec(b0.shape, lambda bi: (0, 0)),
        ],
        out_specs=pl.BlockSpec((1, s, B_W), lambda bi: (bi, 0, 0)),
        out_shape=jax.ShapeDtypeStruct((b, s, B_W), jnp.bfloat16),
        scratch_shapes=[
            pltpu.VMEM((units, 2 * PAIR, B_BAND), jnp.float32),
            pltpu.VMEM((units, slots, 2 * PAIR, B_BAND), jnp.float32),
            pltpu.VMEM((units, slots, 2 * PAIR, LANES), jnp.float32),
            pltpu.VMEM((units, slots, PAIR, 2 * B_BAND), jnp.bfloat16),
        ],
        compiler_params=pltpu.CompilerParams(
            dimension_semantics=("arbitrary",),
            vmem_limit_bytes=VMEM_LIMIT),
        name="attention_b",
    )(zq, zkv, b0)


def _out_proj_kernel(x_ref, ya_ref, yb_ref, p_ref, w_out_f32_ref, ple_g_ref, w_gate_f32_ref,
                     w_proj_f32_ref, fin_g_ref, o_ref, u_ref, w_out_ref, w_gate_ref, w_proj_ref):
    wa = ya_ref.shape[1]
    first_step = pl.program_id(0) == 0
    _cast_weight_once(first_step, w_out_f32_ref, w_out_ref)
    _cast_weight_once(first_step, w_gate_f32_ref, w_gate_ref)
    _cast_weight_once(first_step, w_proj_f32_ref, w_proj_ref)

    def residual(rows):
        h = (x_ref[rows, :]
             + jnp.dot(ya_ref[rows, :], w_out_ref[:wa, :], preferred_element_type=jnp.float32)
             + jnp.dot(yb_ref[rows, :], w_out_ref[wa:, :], preferred_element_type=jnp.float32))
        o_ref[rows, :] = h
        u_ref[rows, :] = (_rms_scale(h) * ple_g_ref[...]).astype(jnp.bfloat16)

    def embed(rows):
        gl = jnp.dot(u_ref[rows, :], w_gate_ref[...], preferred_element_type=jnp.float32)
        gate = 1.0 / (1.0 + jnp.exp(-gl))
        pe = jnp.dot(p_ref[rows, :].astype(jnp.bfloat16), w_proj_ref[...],
                     preferred_element_type=jnp.float32)
        h = o_ref[rows, :] + pe * gate
        o_ref[rows, :] = _rms_scale(h) * fin_g_ref[...]

    subs = [slice(r0, r0 + OUT_SUB_BLOCK) for r0 in range(0, TOK_BLOCK, OUT_SUB_BLOCK)]
    for rows in subs:
        residual(rows)
    for rows in subs:
        embed(rows)


def _out_proj(x2, ya2, yb2, p2, w_out, ple_g, w_gate, w_proj, fin_g):
    n, d = x2.shape
    pd = p2.shape[1]
    tok = lambda i: (i, 0)
    const = lambda i: (0, 0)
    return pl.pallas_call(
        _out_proj_kernel,
        grid=(n // TOK_BLOCK,),
        in_specs=[
            pl.BlockSpec((TOK_BLOCK, d), tok),
            pl.BlockSpec((TOK_BLOCK, ya2.shape[1]), tok),
            pl.BlockSpec((TOK_BLOCK, yb2.shape[1]), tok),
            pl.BlockSpec((TOK_BLOCK, pd), tok),
            pl.BlockSpec((d, d), const, pipeline_mode=pl.Buffered(1)),
            pl.BlockSpec((1, d), const),
            pl.BlockSpec((d, d), const, pipeline_mode=pl.Buffered(1)),
            pl.BlockSpec((pd, d), const, pipeline_mode=pl.Buffered(1)),
            pl.BlockSpec((1, d), const),
        ],
        out_specs=pl.BlockSpec((TOK_BLOCK, d), tok),
        out_shape=jax.ShapeDtypeStruct((n, d), jnp.float32),
        scratch_shapes=[pltpu.VMEM((TOK_BLOCK, d), jnp.bfloat16),
                        pltpu.VMEM((d, d), jnp.bfloat16),
                        pltpu.VMEM((d, d), jnp.bfloat16),
                        pltpu.VMEM((pd, d), jnp.bfloat16)],
        compiler_params=pltpu.CompilerParams(
            dimension_semantics=("arbitrary",),
            vmem_limit_bytes=VMEM_LIMIT),
        name="out_proj",
    )(x2, ya2, yb2, p2, w_out, ple_g, w_gate, w_proj, fin_g)


@jax.jit
def kernel(x, p, norm_g, w_in, sink_a, rel_bias_b, w_out, ple_norm_g, w_ple_proj, w_ple_gate,
           final_norm_g):
    b, s, d = x.shape
    assert norm_g.shape[0] == 1, "the final RMSNorm is fused into the (single) layer's output kernel"
    zqa, zqb, zkva, zkvb = _in_proj(x, norm_g[0][None], w_in[0])
    sink = (sink_a[0].astype(jnp.float32) * LOG2E).reshape(A_HEADS // 2, 1, 2, 1)
    sink = jnp.broadcast_to(sink, (A_HEADS // 2, PAIR, 2, HEAD_DIM)).reshape(A_HEADS // 2, PAIR, LANES)
    ya = _attention_a(zqa, zkva, sink)
    yb = _attention_b(zqb, zkvb, _toeplitz_base_rows(rel_bias_b[0]))
    out = _out_proj(x.reshape(b * s, d), ya.reshape(b * s, -1), yb.reshape(b * s, -1),
                    p[0].reshape(b * s, -1), w_out[0], ple_norm_g[0][None],
                    w_ple_gate[0], w_ple_proj[0], final_norm_g[None])
    return out.reshape(b, s, d)
```
